```python
import jax, jax.numpy as jnp
from jax import lax
import numpy as np

D_MODEL = 1024
BATCH = 32
SEQ = 2048
DEPTH = 4
DEC_BATCH = 8
DEC_SEQ = 16
PAST_LEN = 1024

CHUNK = 64
N_EVEN = (DEPTH + 1) // 2
N_ODD = DEPTH // 2
SC_WIDTH = D_MODEL // 2
SC_CONV = 3
FOX_HEADS = 8
HEAD_DIM = 64
FOX_WIDTH = FOX_HEADS * HEAD_DIM
Q_BLOCK = 128
EVEN_IN = 3 * SC_WIDTH + 3 * FOX_WIDTH + FOX_HEADS
FORGET_BIAS_INIT = 3.0
D_INNER = 2 * D_MODEL
SSM_HEAD_DIM = 64
SSM_HEADS = D_INNER // SSM_HEAD_DIM
SSM_GROUPS = 4
D_STATE = 128
SSM_CONV = 4
CONV_DIM = D_INNER + 2 * SSM_GROUPS * D_STATE
ODD_IN = D_INNER + CONV_DIM + SSM_HEADS
D_FF = 2816
FFN_CONV = 3
EPS = 1e-6
RESID_SCALE = (2 * DEPTH) ** -0.5

kernel_name = 'hybrid_stream_shortconv_fox_ssd_convffn'


def rms_norm(x, gain):
    xf = x.astype(jnp.float32)
    y = xf * lax.rsqrt(jnp.mean(xf * xf, axis=-1, keepdims=True) + EPS)
    return (y * gain.astype(jnp.float32)).astype(x.dtype)


def causal_dwconv(x, state, w, bias=None):
    width = w.shape[0]
    length = x.shape[1]
    xp = jnp.concatenate([state.astype(x.dtype), x], axis=1)
    y = xp[:, 0:length] * w[0]
    for j in range(1, width):
        y = y + xp[:, j:j + length] * w[j]
    if bias is not None:
        y = y + bias
    return y, xp[:, length:]


def fox_attention(q, k, v, logf, n_past):
    b, lq, nh, hd = q.shape
    lk = k.shape[1]
    cum = jnp.cumsum(logf.astype(jnp.float32), axis=1)
    cum_k = jnp.swapaxes(cum, 1, 2)
    cum_q = cum_k[:, :, n_past:]
    qb = Q_BLOCK if lq % Q_BLOCK == 0 else lq
    nb = lq // qb
    q_blocks = jnp.moveaxis(q.reshape(b, nb, qb, nh, hd), 1, 0)
    c_blocks = jnp.moveaxis(cum_q.reshape(b, nh, nb, qb), 2, 0)
    pos_blocks = (n_past + jnp.arange(lq)).reshape(nb, qb)
    kpos = jnp.arange(lk)
    scale = HEAD_DIM ** -0.5

    def block(args):
        qblk, cblk, pblk = args
        s = jnp.einsum('bqhd,bkhd->bhqk', qblk, k).astype(jnp.float32) * scale
        s = s + cblk[..., None] - cum_k[:, :, None, :]
        s = jnp.where(kpos[None, :] <= pblk[:, None], s, -jnp.inf)
        p = jax.nn.softmax(s, axis=-1)
        return jnp.einsum('bhqk,bkhd->bqhd', p.astype(v.dtype), v)

    out = lax.map(block, (q_blocks, c_blocks, pos_blocks))
    return jnp.moveaxis(out, 0, 1).reshape(b, lq, nh, hd)


def ssd_scan(x, dt, a, bmat, cmat, h0, chunk):
    f32 = jnp.float32
    b, length, nh, hp = x.shape
    g, n = bmat.shape[2], bmat.shape[3]
    r = nh // g
    nc = length // chunk
    xc = x.astype(f32).reshape(b, nc, chunk, g, r, hp)
    dtc = dt.astype(f32).reshape(b, nc, chunk, g, r)
    bc = bmat.astype(f32).reshape(b, nc, chunk, g, n)
    cc = cmat.astype(f32).reshape(b, nc, chunk, g, n)
    cum = jnp.cumsum(dtc * a.astype(f32).reshape(g, r), axis=2)
    tri = jnp.tril(jnp.ones((chunk, chunk), dtype=bool))
    seg = cum[:, :, :, None] - cum[:, :, None, :]
    decay = jnp.exp(jnp.where(tri[:, :, None, None], seg, -jnp.inf))
    cb = jnp.einsum('bcqgn,bcsgn->bcqsg', cc, bc)
    mix = cb[..., None] * decay * dtc[:, :, None]
    y_intra = jnp.einsum('bcqsgr,bcsgrp->bcqgrp', mix, xc)
    w_end = jnp.exp(cum[:, :, -1:] - cum) * dtc
    states = jnp.einsum('bcsgn,bcsgr,bcsgrp->bcgrpn', bc, w_end, xc)
    chunk_decay = jnp.exp(cum[:, :, -1])

    def step(h, inp):
        st, dec = inp
        return dec[..., None, None] * h + st, h

    h_last, h_in = lax.scan(step, h0.astype(f32).reshape(b, g, r, hp, n),
                            (jnp.moveaxis(states, 1, 0), jnp.moveaxis(chunk_decay, 1, 0)))
    h_in = jnp.moveaxis(h_in, 0, 1)
    y_inter = jnp.einsum('bcqgn,bcgrpn,bcqgr->bcqgrp', cc, h_in, jnp.exp(cum))
    return (y_intra + y_inter).reshape(b, length, nh, hp), h_last.reshape(b, nh, hp, n)


def even_mixer(h, ck, cv, clogf, sconv_state, w_in, conv_w, q_gain, k_gain, b_f, w_out):
    b, length, _ = h.shape
    proj = h @ w_in
    cuts = [SC_WIDTH, 2 * SC_WIDTH, 3 * SC_WIDTH, 3 * SC_WIDTH + FOX_WIDTH,
            3 * SC_WIDTH + 2 * FOX_WIDTH, 3 * SC_WIDTH + 3 * FOX_WIDTH]
    gate_b, gate_c, u, q, k, v, f_logit = jnp.split(proj, cuts, axis=-1)
    conv_out, new_sconv = causal_dwconv(gate_c * u, sconv_state, conv_w)
    a_out = gate_b * conv_out
    hs = (b, length, FOX_HEADS, HEAD_DIM)
    q = rms_norm(q.reshape(hs), q_gain)
    k = rms_norm(k.reshape(hs), k_gain)
    v = v.reshape(hs)
    logf = jax.nn.log_sigmoid(f_logit.astype(jnp.float32) + b_f.astype(jnp.float32))
    n_past = ck.shape[1]
    k_all = jnp.concatenate([ck.astype(k.dtype), k], axis=1)
    v_all = jnp.concatenate([cv.astype(v.dtype), v], axis=1)
    logf_all = jnp.concatenate([clogf.astype(jnp.float32), logf], axis=1)
    attn = fox_attention(q, k_all, v_all, logf_all, n_past).reshape(b, length, FOX_WIDTH)
    out = jnp.concatenate([a_out, attn.astype(a_out.dtype)], axis=-1) @ w_out
    return (out, k.astype(ck.dtype), v.astype(cv.dtype), logf.astype(clogf.dtype),
            new_sconv.astype(sconv_state.dtype))


def odd_mixer(h, conv_state, ssm_state, w_in, conv_w, conv_b, dt_bias, a_log, d_skip, norm_w, w_out):
    f32 = jnp.float32
    b, length, _ = h.shape
    proj = h @ w_in
    z, xbc, dt_raw = jnp.split(proj, [D_INNER, D_INNER + CONV_DIM], axis=-1)
    xbc, new_conv = causal_dwconv(xbc, conv_state, conv_w, conv_b)
    xbc = jax.nn.silu(xbc)
    xs, bm, cm = jnp.split(xbc, [D_INNER, D_INNER + SSM_GROUPS * D_STATE], axis=-1)
    dt = jax.nn.softplus(dt_raw.astype(f32) + dt_bias.astype(f32))
    a = -jnp.exp(a_log.astype(f32))
    xs = xs.reshape(b, length, SSM_HEADS, SSM_HEAD_DIM)
    chunk = CHUNK if length % CHUNK == 0 else length
    y, h_last = ssd_scan(xs, dt, a,
                         bm.reshape(b, length, SSM_GROUPS, D_STATE),
                         cm.reshape(b, length, SSM_GROUPS, D_STATE), ssm_state, chunk)
    y = y + d_skip.astype(f32)[:, None] * xs.astype(f32)
    y = y.reshape(b, length, D_INNER) * jax.nn.silu(z.astype(f32))
    y = rms_norm(y.reshape(b, length, SSM_GROUPS, D_INNER // SSM_GROUPS),
                 norm_w.reshape(SSM_GROUPS, D_INNER // SSM_GROUPS)).reshape(b, length, D_INNER)
    out = y.astype(h.dtype) @ w_out
    return out, new_conv.astype(conv_state.dtype), h_last.astype(ssm_state.dtype)


def conv_ffn(h, state, w_up, conv_w, conv_b, w_down):
    a, g = jnp.split(h @ w_up, 2, axis=-1)
    a, new_state = causal_dwconv(a, state, conv_w, conv_b)
    return (jax.nn.silu(a) * g) @ w_down, new_state.astype(state.dtype)


def trunk(x, cache_k, cache_v, cache_logf, st_sconv, st_ssm_conv, st_ssm, st_ffn,
          norm_mix, norm_ffn, w_in_even, conv_a_w, q_norm, k_norm, b_forget, w_out_even,
          w_in_odd, ssm_conv_w, ssm_conv_b, dt_bias, a_log, d_skip, ssm_norm, w_out_odd,
          w_up, ffn_conv_w, ffn_conv_b, w_down):
    nk, nv, nlf, nsc, nsmc, nsm, nff = [], [], [], [], [], [], []
    for i in range(DEPTH):
        j = i // 2
        h = rms_norm(x, norm_mix[i])
        if i % 2 == 0:
            out, k, v, lf, sc = even_mixer(h, cache_k[j], cache_v[j], cache_logf[j], st_sconv[j],
                                           w_in_even[j], conv_a_w[j], q_norm[j], k_norm[j],
                                           b_forget[j], w_out_even[j])
            nk.append(k); nv.append(v); nlf.append(lf); nsc.append(sc)
        else:
            out, cs, ss = odd_mixer(h, st_ssm_conv[j], st_ssm[j], w_in_odd[j], ssm_conv_w[j],
                                    ssm_conv_b[j], dt_bias[j], a_log[j], d_skip[j], ssm_norm[j],
                                    w_out_odd[j])
            nsmc.append(cs); nsm.append(ss)
        x = x + out
        f, fs = conv_ffn(rms_norm(x, norm_ffn[i]), st_ffn[i], w_up[i], ffn_conv_w[i],
                         ffn_conv_b[i], w_down[i])
        nff.append(fs)
        x = x + f
    return (x, jnp.stack(nk), jnp.stack(nv), jnp.stack(nlf), jnp.stack(nsc),
            jnp.stack(nsmc), jnp.stack(nsm), jnp.stack(nff))


def setup_inputs(seed: int = 0) -> dict:
    key = jax.random.key(seed)
    ks = jax.random.split(key, 32)
    f32 = jnp.float32

    def nrm(k, shape, scale):
        return jax.random.normal(k, shape, f32) * scale

    dt0 = jnp.exp(jax.random.uniform(ks[20], (N_ODD, SSM_HEADS), f32,
                                     minval=np.log(1e-3), maxval=np.log(1e-1)))
    return {
        'x_prompt': nrm(ks[0], (BATCH, SEQ, D_MODEL), 1.0),
        'x_sample': nrm(ks[1], (DEC_BATCH, DEC_SEQ, D_MODEL), 1.0),
        'cache_fox_k': nrm(ks[2], (N_EVEN, DEC_BATCH, PAST_LEN, FOX_HEADS, HEAD_DIM), 1.0),
        'cache_fox_v': nrm(ks[3], (N_EVEN, DEC_BATCH, PAST_LEN, FOX_HEADS, HEAD_DIM), 1.0),
        'cache_fox_logf': jax.nn.log_sigmoid(FORGET_BIAS_INIT + nrm(ks[4], (N_EVEN, DEC_BATCH, PAST_LEN, FOX_HEADS), 1.0)),
        'state_sconv': nrm(ks[5], (N_EVEN, DEC_BATCH, SC_CONV - 1, SC_WIDTH), 1.0),
        'state_ssm_conv': nrm(ks[6], (N_ODD, DEC_BATCH, SSM_CONV - 1, CONV_DIM), 1.0),
        'state_ssm': nrm(ks[7], (N_ODD, DEC_BATCH, SSM_HEADS, SSM_HEAD_DIM, D_STATE), 0.1),
        'state_ffn_conv': nrm(ks[8], (DEPTH, DEC_BATCH, FFN_CONV - 1, D_FF), 1.0),
        'norm_mix': 1.0 + nrm(ks[9], (DEPTH, D_MODEL), 0.02),
        'norm_ffn': 1.0 + nrm(ks[10], (DEPTH, D_MODEL), 0.02),
        'w_in_even': nrm(ks[11], (N_EVEN, D_MODEL, EVEN_IN), D_MODEL ** -0.5),
        'conv_a_w': nrm(ks[12], (N_EVEN, SC_CONV, SC_WIDTH), SC_CONV ** -0.5),
        'q_norm': 1.0 + nrm(ks[13], (N_EVEN, HEAD_DIM), 0.02),
        'k_norm': 1.0 + nrm(ks[14], (N_EVEN, HEAD_DIM), 0.02),
        'b_forget': FORGET_BIAS_INIT + nrm(ks[15], (N_EVEN, FOX_HEADS), 0.1),
        'w_out_even': nrm(ks[16], (N_EVEN, SC_WIDTH + FOX_WIDTH, D_MODEL), (SC_WIDTH + FOX_WIDTH) ** -0.5 * RESID_SCALE),
        'w_in_odd': nrm(ks[17], (N_ODD, D_MODEL, ODD_IN), D_MODEL ** -0.5),
        'ssm_conv_w': nrm(ks[18], (N_ODD, SSM_CONV, CONV_DIM), SSM_CONV ** -0.5),
        'ssm_conv_b': nrm(ks[19], (N_ODD, CONV_DIM), 0.02),
        'dt_bias': dt0 + jnp.log(-jnp.expm1(-dt0)),
        'a_log': jnp.log(jax.random.uniform(ks[21], (N_ODD, SSM_HEADS), f32, minval=1.0, maxval=16.0)),
        'd_skip': 1.0 + nrm(ks[22], (N_ODD, SSM_HEADS), 0.1),
        'ssm_norm': 1.0 + nrm(ks[23], (N_ODD, D_INNER), 0.02),
        'w_out_odd': nrm(ks[24], (N_ODD, D_INNER, D_MODEL), D_INNER ** -0.5 * RESID_SCALE),
        'w_up': nrm(ks[25], (DEPTH, D_MODEL, 2 * D_FF), D_MODEL ** -0.5),
        'ffn_conv_w': nrm(ks[26], (DEPTH, FFN_CONV, D_FF), FFN_CONV ** -0.5),
        'ffn_conv_b': nrm(ks[27], (DEPTH, D_FF), 0.02),
        'w_down': nrm(ks[28], (DEPTH, D_FF, D_MODEL), D_FF ** -0.5 * RESID_SCALE),
    }


def reference(x_prompt, x_sample, cache_fox_k, cache_fox_v, cache_fox_logf, state_sconv,
              state_ssm_conv, state_ssm, state_ffn_conv, norm_mix, norm_ffn, w_in_even, conv_a_w,
              q_norm, k_norm, b_forget, w_out_even, w_in_odd, ssm_conv_w, ssm_conv_b, dt_bias,
              a_log, d_skip, ssm_norm, w_out_odd, w_up, ffn_conv_w, ffn_conv_b, w_down):
    bp = x_prompt.shape[0]
    e_k = jnp.zeros((N_EVEN, bp, 0, FOX_HEADS, HEAD_DIM), cache_fox_k.dtype)
    e_v = jnp.zeros((N_EVEN, bp, 0, FOX_HEADS, HEAD_DIM), cache_fox_v.dtype)
    e_lf = jnp.zeros((N_EVEN, bp, 0, FOX_HEADS), cache_fox_logf.dtype)
    z_sc = jnp.zeros((N_EVEN, bp) + state_sconv.shape[2:], state_sconv.dtype)
    z_smc = jnp.zeros((N_ODD, bp) + state_ssm_conv.shape[2:], state_ssm_conv.dtype)
    z_sm = jnp.zeros((N_ODD, bp) + state_ssm.shape[2:], state_ssm.dtype)
    z_ff = jnp.zeros((DEPTH, bp) + state_ffn_conv.shape[2:], state_ffn_conv.dtype)
    y_prompt, p_fox_k, p_fox_v, p_fox_logf, p_sconv, p_ssm_conv, p_ssm, p_ffn_conv = trunk(
        x_prompt, e_k, e_v, e_lf, z_sc, z_smc, z_sm, z_ff,
        norm_mix, norm_ffn, w_in_even, conv_a_w, q_norm, k_norm, b_forget, w_out_even,
        w_in_odd, ssm_conv_w, ssm_conv_b, dt_bias, a_log, d_skip, ssm_norm, w_out_odd,
        w_up, ffn_conv_w, ffn_conv_b, w_down)
    y_sample, s_fox_k, s_fox_v, s_fox_logf, s_sconv, s_ssm_conv, s_ssm, s_ffn_conv = trunk(
        x_sample, cache_fox_k, cache_fox_v, cache_fox_logf, state_sconv, state_ssm_conv,
        state_ssm, state_ffn_conv,
        norm_mix, norm_ffn, w_in_even, conv_a_w, q_norm, k_norm, b_forget, w_out_even,
        w_in_odd, ssm_conv_w, ssm_conv_b, dt_bias, a_log, d_skip, ssm_norm, w_out_odd,
        w_up, ffn_conv_w, ffn_conv_b, w_down)
    return (y_prompt, y_sample, p_fox_k, p_fox_v, p_fox_logf, p_sconv, p_ssm_conv, p_ssm, p_ffn_conv,
            s_fox_k, s_fox_v, s_fox_logf, s_sconv, s_ssm_conv, s_ssm, s_ffn_conv)
```

```python
import functools
import math

import jax
import jax.numpy as jnp
from jax import lax
from jax.experimental import pallas as pl
from jax.experimental.pallas import tpu as pltpu

F32 = jnp.float32
BF16 = jnp.bfloat16
EPS = 1e-6

LANES = 128
SUBLANES = 8
MXU_DIM = 256

HEAD_DIM = 64
FOX_HEADS = 8
FOX_WIDTH = FOX_HEADS * HEAD_DIM
SC_WIDTH = 512
SSM_HEAD_DIM = 64
SSM_GROUPS = 4
D_STATE = 128
SSD_CHUNK = 128
VMEM_LIMIT = 60 * 1024 * 1024


def _dot(a, b):
    return jnp.dot(a, b, preferred_element_type=F32)


def _rms(x, gain):
    ms = jnp.mean(x * x, axis=-1, keepdims=True)
    return x * lax.rsqrt(ms + EPS) * gain


def _silu(x):
    return x / (1.0 + jnp.exp(-x))


def _softplus(x):
    return jnp.maximum(x, 0.0) + jnp.log1p(jnp.exp(-jnp.abs(x)))


def _pad_rows(a, rows):
    if a.shape[0] == rows:
        return a
    return jnp.concatenate([a, jnp.zeros((rows - a.shape[0], a.shape[1]), a.dtype)], axis=0)


def _lane_cumsum(x):
    n = x.shape[1]
    lane = lax.broadcasted_iota(jnp.int32, x.shape, 1)
    s = 1
    while s < n:
        x = x + jnp.where(lane >= s, pltpu.roll(x, s, axis=1), 0.0)
        s *= 2
    return x


def _causal_conv(a, tail, w, bias):
    width = w.shape[0]
    row = lax.broadcasted_iota(jnp.int32, a.shape, 0)
    y = a * w[width - 1:width]
    for d in range(1, width):
        sh = pltpu.roll(a, d, axis=0)
        for r in range(d):
            sh = jnp.where(row == r, tail[SUBLANES - d + r:SUBLANES - d + r + 1], sh)
        y = y + sh * w[width - 1 - d:width - d]
    if bias is not None:
        y = y + bias
    return y


def _pair_lane_mask(shape, e):
    lane = lax.broadcasted_iota(jnp.int32, shape, 1)
    return (lane < HEAD_DIM) if e == 0 else (lane >= HEAD_DIM)


def _expand_heads(v, g):
    first_half = _pair_lane_mask((v.shape[0], LANES), 0)
    pieces = []
    for j in range(4):
        h0 = 8 * g + 2 * j
        pieces.append(jnp.where(first_half, v[:, h0:h0 + 1], v[:, h0 + 1:h0 + 2]))
    return jnp.concatenate(pieces, axis=1)


def _ffn_body(x_ref, st_ref, g_ref, wup_ref, cw_ref, cb_ref, wdn_ref, o_ref, nst_ref, acc_ref,
              *, lb, dff, cwid):
    @pl.when(pl.program_id(1) == 0)
    def _():
        nst_ref[...] = st_ref[...]

    x = x_ref[0]
    h = _rms(x, g_ref[...]).astype(BF16)
    for c in range(dff // cwid):
        lo = c * cwid
        a = _dot(h, wup_ref[:, lo:lo + cwid])
        gate = _dot(h, wup_ref[:, dff + lo:dff + lo + cwid])
        y = _causal_conv(a, nst_ref[0, :, lo:lo + cwid], cw_ref[:, lo:lo + cwid], cb_ref[:, lo:lo + cwid])
        nst_ref[0, :, lo:lo + cwid] = a[lb - SUBLANES:lb]
        y = (_silu(y) * gate).astype(BF16)
        d = _dot(y, wdn_ref[lo:lo + cwid, :])
        if c == 0:
            acc_ref[...] = d
        else:
            acc_ref[...] += d
    o_ref[0] = x + acc_ref[...]


def _ffn_call(x, st8, gain, wup, cw, cb, wdn, lb):
    b, l, d = x.shape
    dff = wdn.shape[0]
    const = lambda i, j: (0, 0)
    return pl.pallas_call(
        functools.partial(_ffn_body, lb=lb, dff=dff, cwid=MXU_DIM),
        grid=(b, l // lb),
        in_specs=[
            pl.BlockSpec((1, lb, d), lambda i, j: (i, j, 0)),
            pl.BlockSpec((1, SUBLANES, dff), lambda i, j: (i, 0, 0)),
            pl.BlockSpec((1, d), const),
            pl.BlockSpec((d, 2 * dff), const),
            pl.BlockSpec(cw.shape, const),
            pl.BlockSpec((1, dff), const),
            pl.BlockSpec((dff, d), const),
        ],
        out_specs=[
            pl.BlockSpec((1, lb, d), lambda i, j: (i, j, 0)),
            pl.BlockSpec((1, SUBLANES, dff), lambda i, j: (i, 0, 0)),
        ],
        out_shape=[jax.ShapeDtypeStruct(x.shape, F32), jax.ShapeDtypeStruct((b, SUBLANES, dff), F32)],
        scratch_shapes=[pltpu.VMEM((lb, d), F32)],
        compiler_params=pltpu.CompilerParams(
            dimension_semantics=("arbitrary", "arbitrary"), vmem_limit_bytes=VMEM_LIMIT),
        name="conv_ffn",
    )(x, st8, gain, wup, cw, cb, wdn)


def _head_norm(q, gain):
    r = lax.shift_right_logical(lax.broadcasted_iota(jnp.int32, (MXU_DIM, MXU_DIM), 0), 6)
    c = lax.shift_right_logical(lax.broadcasted_iota(jnp.int32, (MXU_DIM, MXU_DIM), 1), 6)
    gmat = jnp.where(r == c, 1.0 / HEAD_DIM, 0.0).astype(BF16)
    qq = (q * q).astype(BF16)
    ms = jnp.concatenate([_dot(qq[:, i * MXU_DIM:(i + 1) * MXU_DIM], gmat)
                          for i in range(q.shape[1] // MXU_DIM)], axis=1)
    return q * lax.rsqrt(ms + EPS) * gain


def _even_body(*refs, lb, seq, past, tk, wd):
    if past:
        (x_ref, ck_ref, cv_ref, clf_ref, st_ref, g_ref, win_ref, wf_ref, cw_ref, qg_ref, kg_ref, bf_ref,
         wout_ref, xo_ref, ko_ref, vo_ref, lfo_ref, nst_ref, kt_s, v_s, cumt_s, ccar_s) = refs
    else:
        (x_ref, st_ref, g_ref, win_ref, wf_ref, cw_ref, qg_ref, kg_ref, bf_ref,
         wout_ref, xo_ref, ko_ref, vo_ref, lfo_ref, nst_ref, kt_s, v_s, cumt_s, ccar_s) = refs
    blk = pl.program_id(1)

    @pl.when(blk == 0)
    def _():
        nst_ref[...] = st_ref[...]
        ccar_s[...] = jnp.zeros(ccar_s.shape, F32)
        if past:
            kt_s[:, 0:past] = ck_ref[0].T.astype(BF16)
            v_s[0:past, :] = cv_ref[0].astype(BF16)
            cpast = _lane_cumsum(clf_ref[0])
            cumt_s[:, 0:past] = cpast
            ccar_s[0:SUBLANES, :] = jnp.broadcast_to(cpast[:, past - 1:past], (SUBLANES, LANES))

    x = x_ref[0]
    h = _rms(x, g_ref[...]).astype(BF16)
    gate_b, gate_c, u, q, k, v = [_dot(h, win_ref[:, i * SC_WIDTH:(i + 1) * SC_WIDTH]) for i in range(6)]
    f_logit = _dot(h, wf_ref[...])

    cu = gate_c * u
    a_out = gate_b * _causal_conv(cu, nst_ref[0], cw_ref[...], None)
    nst_ref[0] = cu[lb - SUBLANES:lb]

    qn = _head_norm(q, qg_ref[...])
    kn = _head_norm(k, kg_ref[...])
    ko_ref[0] = kn
    vo_ref[0] = v
    lane = lax.broadcasted_iota(jnp.int32, (lb, LANES), 1)
    logf = jnp.where(lane < FOX_HEADS, -_softplus(-(f_logit + bf_ref[...])), 0.0)
    lfo_ref[0] = logf[:, 0:FOX_HEADS]

    pos0 = past if seq == lb else pl.multiple_of(past + blk * lb, LANES)
    cumt_blk = _lane_cumsum(_pad_rows(logf, wd).T) + ccar_s[:, 0:1]
    ccar_s[...] = jnp.broadcast_to(cumt_blk[:, lb - 1:lb], (LANES, LANES))
    cumt_s[:, pl.ds(pos0, wd)] = cumt_blk[0:SUBLANES]
    cum = cumt_blk.T[0:lb]
    kt_s[:, pl.ds(pos0, wd)] = _pad_rows(kn, wd).T.astype(BF16)
    v_s[pl.ds(pos0, wd), :] = _pad_rows(v, wd).astype(BF16)

    qs = qn * (HEAD_DIM ** -0.5)
    causal = (lax.broadcasted_iota(jnp.int32, (lb, wd), 1) <= lax.broadcasted_iota(jnp.int32, (lb, wd), 0))
    n_prev = past // tk if seq == lb else past // tk + blk * (lb // tk)
    pair_outs = []
    for g in range(FOX_HEADS // 2):
        rows = slice(LANES * g, LANES * (g + 1))
        q_pair = qs[:, rows]
        pair_o = None
        for e in range(2):
            hh = 2 * g + e
            qm = jnp.where(_pair_lane_mask((lb, LANES), e), q_pair, 0.0).astype(BF16)
            cq = cum[:, hh:hh + 1]
            s = _dot(qm, kt_s[rows, pl.ds(pos0, wd)]) + (cq - cumt_s[hh:hh + 1, pl.ds(pos0, wd)])
            s = jnp.where(causal, s, -jnp.inf)
            m = jnp.max(s, axis=1, keepdims=True)
            p = jnp.exp(s - m)
            den = jnp.sum(p, axis=1, keepdims=True)
            acc = _dot(p.astype(BF16), v_s[pl.ds(pos0, wd), rows])

            def prev_block(j, carry, qm=qm, cq=cq, hh=hh, rows=rows):
                m, den, acc = carry
                off = pl.multiple_of(j * tk, tk)
                s = _dot(qm, kt_s[rows, pl.ds(off, tk)]) + (cq - cumt_s[hh:hh + 1, pl.ds(off, tk)])
                m_new = jnp.maximum(m, jnp.max(s, axis=1, keepdims=True))
                alpha = jnp.exp(m - m_new)
                p = jnp.exp(s - m_new)
                den = alpha * den + jnp.sum(p, axis=1, keepdims=True)
                acc = alpha * acc + _dot(p.astype(BF16), v_s[pl.ds(off, tk), rows])
                return m_new, den, acc

            m, den, acc = lax.fori_loop(0, n_prev, prev_block, (m, den, acc))
            o = acc / den
            pair_o = o if e == 0 else jnp.where(_pair_lane_mask((lb, LANES), 0), pair_o, o)
        pair_outs.append(pair_o)
    attn = jnp.concatenate(pair_outs, axis=1)

    cat = jnp.concatenate([a_out, attn], axis=1).astype(BF16)
    xo_ref[0] = x + _dot(cat, wout_ref[...])


def _even_call(x, cache, st8, gain, win, wf, cw, qg, kg, bf, wout, lb):
    b, l, d = x.shape
    past = 0 if cache is None else cache[0].shape[1]
    wd = max(lb, LANES)
    tk = lb if past == 0 else min(past, 512)
    assert l % lb == 0 and past % tk == 0 and (l == lb or lb % tk == 0)
    total = past + max(l, wd)
    const = lambda i, j: (0, 0)
    tok = lambda width: pl.BlockSpec((1, lb, width), lambda i, j: (i, j, 0))
    in_specs = [tok(d)]
    args = [x]
    if past:
        ck, cv, clft = cache
        in_specs += [pl.BlockSpec((1, past, FOX_WIDTH), lambda i, j: (i, 0, 0)),
                     pl.BlockSpec((1, past, FOX_WIDTH), lambda i, j: (i, 0, 0)),
                     pl.BlockSpec((1, SUBLANES, past), lambda i, j: (i, 0, 0))]
        args += [ck, cv, clft]
    in_specs += [
        pl.BlockSpec((1, SUBLANES, SC_WIDTH), lambda i, j: (i, 0, 0)),
        pl.BlockSpec((1, d), const),
        pl.BlockSpec(win.shape, const),
        pl.BlockSpec(wf.shape, const),
        pl.BlockSpec(cw.shape, const),
        pl.BlockSpec((1, FOX_WIDTH), const),
        pl.BlockSpec((1, FOX_WIDTH), const),
        pl.BlockSpec((1, LANES), const),
        pl.BlockSpec(wout.shape, const),
    ]
    args += [st8, gain, win, wf, cw, qg, kg, bf, wout]
    return pl.pallas_call(
        functools.partial(_even_body, lb=lb, seq=l, past=past, tk=tk, wd=wd),
        grid=(b, l // lb),
        in_specs=in_specs,
        out_specs=[tok(d), tok(FOX_WIDTH), tok(FOX_WIDTH), tok(FOX_HEADS),
                   pl.BlockSpec((1, SUBLANES, SC_WIDTH), lambda i, j: (i, 0, 0))],
        out_shape=[jax.ShapeDtypeStruct(x.shape, F32),
                   jax.ShapeDtypeStruct((b, l, FOX_WIDTH), F32),
                   jax.ShapeDtypeStruct((b, l, FOX_WIDTH), F32),
                   jax.ShapeDtypeStruct((b, l, FOX_HEADS), F32),
                   jax.ShapeDtypeStruct((b, SUBLANES, SC_WIDTH), F32)],
        scratch_shapes=[pltpu.VMEM((FOX_WIDTH, total), BF16),
                        pltpu.VMEM((total, FOX_WIDTH), BF16),
                        pltpu.VMEM((SUBLANES, total), F32),
                        pltpu.VMEM((LANES, LANES), F32)],
        compiler_params=pltpu.CompilerParams(
            dimension_semantics=("arbitrary", "arbitrary"), vmem_limit_bytes=VMEM_LIMIT),
        name="even_mixer",
    )(*args)


def _ssd_chunk(da, dt, xs, bm, cm, z, s_ref, dsk_ref, nw_ref):
    q = da.shape[0]
    cumt = _lane_cumsum(da.T)
    cum = cumt.T
    dtt = dt.T
    c_last = cum[q - 1:q, :]
    w_end = jnp.exp(c_last - cum) * dt
    e_cum = jnp.exp(cum)
    e_last = jnp.exp(c_last)
    tri = (lax.broadcasted_iota(jnp.int32, (q, q), 1) <= lax.broadcasted_iota(jnp.int32, (q, q), 0))
    first_half = _pair_lane_mask((q, LANES), 0)
    outs = []
    for g in range(SSM_GROUPS):
        b_g = bm[:, D_STATE * g:D_STATE * (g + 1)]
        c_g = cm[:, D_STATE * g:D_STATE * (g + 1)].astype(BF16)
        cols = slice(4 * LANES * g, 4 * LANES * (g + 1))
        x_g = xs[:, cols]
        cb = lax.dot_general(c_g, b_g.astype(BF16), (((1,), (1,)), ((), ())), preferred_element_type=F32)
        s_old = s_ref[g]
        y_inter = _dot(c_g, s_old.astype(BF16)) * _expand_heads(e_cum, g)
        xw = (x_g * _expand_heads(w_end, g)).astype(BF16)
        s_ref[g] = s_old * _expand_heads(e_last, g) + _dot(b_g.T.astype(BF16), xw)
        pieces = []
        for j in range(4):
            mixes = []
            for e in range(2):
                hh = 8 * g + 2 * j + e
                seg = cum[:, hh:hh + 1] - cumt[hh:hh + 1, :]
                decay = jnp.exp(jnp.where(tri, seg, -jnp.inf))
                mixes.append((cb * decay * dtt[hh:hh + 1, :]).astype(BF16))
            x_pair = x_g[:, LANES * j:LANES * (j + 1)]
            x_split = jnp.concatenate([jnp.where(first_half, x_pair, 0.0),
                                       jnp.where(first_half, 0.0, x_pair)], axis=0).astype(BF16)
            pieces.append(_dot(jnp.concatenate(mixes, axis=1), x_split))
        y = jnp.concatenate(pieces, axis=1) + y_inter + dsk_ref[:, cols] * x_g
        y = y * _silu(z[:, cols])
        outs.append((_rms(y, nw_ref[:, cols])).astype(BF16))
    return jnp.concatenate(outs, axis=1)


def _odd_body(*refs, lb, has_state):
    if has_state:
        (x_ref, cst_ref, sst_ref, g_ref, wz_ref, wx_ref, wdt_ref, cw_ref, cb_ref, dtb_ref, alog_ref,
         dsk_ref, nw_ref, wout_ref, xo_ref, ncs_ref, nss_ref, s_s, yn_s, z_s, xbc_s, dt_s, da_s) = refs
    else:
        (x_ref, cst_ref, g_ref, wz_ref, wx_ref, wdt_ref, cw_ref, cb_ref, dtb_ref, alog_ref,
         dsk_ref, nw_ref, wout_ref, xo_ref, ncs_ref, nss_ref, s_s, yn_s, z_s, xbc_s, dt_s, da_s) = refs
    blk = pl.program_id(1)
    hp = 4 * LANES
    heads_per_group = hp // SSM_HEAD_DIM

    @pl.when(blk == 0)
    def _():
        ncs_ref[...] = cst_ref[...]
        for g in range(SSM_GROUPS):
            if has_state:
                s_s[g] = sst_ref[0, heads_per_group * g:heads_per_group * (g + 1)].reshape(hp, D_STATE).T
            else:
                s_s[g] = jnp.zeros((D_STATE, hp), F32)

    x = x_ref[0]
    h = _rms(x, g_ref[...]).astype(BF16)
    z_s[...] = _dot(h, wz_ref[...])
    for c in range(wx_ref.shape[1] // hp):
        cols = slice(hp * c, hp * (c + 1))
        raw = _dot(h, wx_ref[:, cols])
        y = _causal_conv(raw, ncs_ref[0, :, cols], cw_ref[:, cols], cb_ref[:, cols])
        ncs_ref[0, :, cols] = raw[lb - SUBLANES:lb]
        xbc_s[:, cols] = _silu(y)
    dt = _softplus(_dot(h, wdt_ref[...]) + dtb_ref[...])
    dt_s[...] = dt
    da_s[...] = dt * (-jnp.exp(alog_ref[...]))

    d_inner = z_s.shape[1]
    bc_w = SSM_GROUPS * D_STATE
    q = SSD_CHUNK
    if lb >= q:
        def chunk(c, carry):
            r = pl.ds(pl.multiple_of(c * q, q), q)
            yn_s[r, :] = _ssd_chunk(da_s[r, :], dt_s[r, :], xbc_s[r, 0:d_inner],
                                    xbc_s[r, d_inner:d_inner + bc_w], xbc_s[r, d_inner + bc_w:d_inner + 2 * bc_w],
                                    z_s[r, :], s_s, dsk_ref, nw_ref)
            return carry
        lax.fori_loop(0, lb // q, chunk, 0)
    else:
        yn = _ssd_chunk(_pad_rows(da_s[...], q), _pad_rows(dt_s[...], q), _pad_rows(xbc_s[:, 0:d_inner], q),
                        _pad_rows(xbc_s[:, d_inner:d_inner + bc_w], q),
                        _pad_rows(xbc_s[:, d_inner + bc_w:d_inner + 2 * bc_w], q),
                        _pad_rows(z_s[...], q), s_s, dsk_ref, nw_ref)
        yn_s[...] = yn[0:lb]
    xo_ref[0] = x + _dot(yn_s[...], wout_ref[...])

    @pl.when(blk == pl.num_programs(1) - 1)
    def _():
        for g in range(SSM_GROUPS):
            nss_ref[0, heads_per_group * g:heads_per_group * (g + 1)] = (
                s_s[g].T.reshape(heads_per_group, SSM_HEAD_DIM, D_STATE))


def _odd_call(x, cst8, sst, gain, wz, wx, wdt, cw, cb, dtb, alog, dsk, nw, wout, lb):
    b, l, d = x.shape
    d_inner = wz.shape[1]
    conv_dim = wx.shape[1]
    heads = d_inner // SSM_HEAD_DIM
    has_state = sst is not None
    const = lambda i, j: (0, 0)
    in_specs = [pl.BlockSpec((1, lb, d), lambda i, j: (i, j, 0)),
                pl.BlockSpec((1, SUBLANES, conv_dim), lambda i, j: (i, 0, 0))]
    args = [x, cst8]
    if has_state:
        in_specs.append(pl.BlockSpec((1, heads, SSM_HEAD_DIM, D_STATE), lambda i, j: (i, 0, 0, 0)))
        args.append(sst)
    in_specs += [
        pl.BlockSpec((1, d), const),
        pl.BlockSpec(wz.shape, const),
        pl.BlockSpec(wx.shape, const),
        pl.BlockSpec(wdt.shape, const),
        pl.BlockSpec(cw.shape, const),
        pl.BlockSpec((1, conv_dim), const),
        pl.BlockSpec((1, LANES), const),
        pl.BlockSpec((1, LANES), const),
        pl.BlockSpec((1, d_inner), const),
        pl.BlockSpec((1, d_inner), const),
        pl.BlockSpec(wout.shape, const),
    ]
    args += [gain, wz, wx, wdt, cw, cb, dtb, alog, dsk, nw, wout]
    return pl.pallas_call(
        functools.partial(_odd_body, lb=lb, has_state=has_state),
        grid=(b, l // lb),
        in_specs=in_specs,
        out_specs=[pl.BlockSpec((1, lb, d), lambda i, j: (i, j, 0)),
                   pl.BlockSpec((1, SUBLANES, conv_dim), lambda i, j: (i, 0, 0)),
                   pl.BlockSpec((1, heads, SSM_HEAD_DIM, D_STATE), lambda i, j: (i, 0, 0, 0))],
        out_shape=[jax.ShapeDtypeStruct(x.shape, F32),
                   jax.ShapeDtypeStruct((b, SUBLANES, conv_dim), F32),
                   jax.ShapeDtypeStruct((b, heads, SSM_HEAD_DIM, D_STATE), F32)],
        scratch_shapes=[pltpu.VMEM((SSM_GROUPS, D_STATE, d_inner // SSM_GROUPS), F32),
                        pltpu.VMEM((lb, d_inner), BF16),
                        pltpu.VMEM((lb, d_inner), F32),
                        pltpu.VMEM((lb, conv_dim), F32),
                        pltpu.VMEM((lb, LANES), F32),
                        pltpu.VMEM((lb, LANES), F32)],
        compiler_params=pltpu.CompilerParams(
            dimension_semantics=("arbitrary", "arbitrary"), vmem_limit_bytes=VMEM_LIMIT),
        name="odd_mixer",
    )(*args)


def _tail8(state):
    return jnp.pad(state, ((0, 0), (SUBLANES - state.shape[1], 0), (0, 0)))


def _pad_lanes(a):
    return jnp.pad(a, [(0, 0)] * (a.ndim - 1) + [(0, LANES - a.shape[-1])])


def _prepare(p):
    n_even = p["w_in_even"].shape[0]
    n_odd = p["w_in_odd"].shape[0]
    main = 3 * SC_WIDTH + 3 * FOX_WIDTH
    d_inner = p["w_out_odd"].shape[1]
    conv_dim = p["ssm_conv_w"].shape[2]
    row = lambda a: a.reshape(1, -1)
    even = [dict(
        win=p["w_in_even"][j, :, :main].astype(BF16),
        wf=_pad_lanes(p["w_in_even"][j, :, main:]).astype(BF16),
        cw=p["conv_a_w"][j],
        qg=row(jnp.tile(p["q_norm"][j], FOX_HEADS)),
        kg=row(jnp.tile(p["k_norm"][j], FOX_HEADS)),
        bf=_pad_lanes(row(p["b_forget"][j])),
        wout=p["w_out_even"][j].astype(BF16),
    ) for j in range(n_even)]
    odd = [dict(
        wz=p["w_in_odd"][j, :, :d_inner].astype(BF16),
        wx=p["w_in_odd"][j, :, d_inner:d_inner + conv_dim].astype(BF16),
        wdt=_pad_lanes(p["w_in_odd"][j, :, d_inner + conv_dim:]).astype(BF16),
        cw=p["ssm_conv_w"][j],
        cb=row(p["ssm_conv_b"][j]),
        dtb=_pad_lanes(row(p["dt_bias"][j])),
        alog=_pad_lanes(row(p["a_log"][j])),
        dsk=row(jnp.repeat(p["d_skip"][j], SSM_HEAD_DIM)),
        nw=row(p["ssm_norm"][j]),
        wout=p["w_out_odd"][j].astype(BF16),
    ) for j in range(n_odd)]
    ffn = [dict(
        gain=row(p["norm_ffn"][i]),
        wup=p["w_up"][i].astype(BF16),
        cw=p["ffn_conv_w"][i],
        cb=row(p["ffn_conv_b"][i]),
        wdn=p["w_down"][i].astype(BF16),
    ) for i in range(p["w_up"].shape[0])]
    mix_gain = [row(p["norm_mix"][i]) for i in range(p["norm_mix"].shape[0])]
    return even, odd, ffn, mix_gain


def _trunk(x, cache_k, cache_v, cache_logf, st_sconv, st_ssm_conv, st_ssm, st_ffn, prep, lb):
    even, odd, ffn, mix_gain = prep
    b, l, _ = x.shape
    nk, nv, nlf, nsc, nsmc, nsm, nff = [], [], [], [], [], [], []
    for i in range(len(ffn)):
        j = i // 2
        if i % 2 == 0:
            w = even[j]
            cache = None
            if cache_k is not None:
                clft = jnp.swapaxes(cache_logf[j], 1, 2)
                cache = (cache_k[j].reshape(b, -1, FOX_WIDTH), cache_v[j].reshape(b, -1, FOX_WIDTH), clft)
            x, k, v, lf, sc = _even_call(x, cache, _tail8(st_sconv[j]), mix_gain[i], w["win"], w["wf"], w["cw"],
                                         w["qg"], w["kg"], w["bf"], w["wout"], lb)
            nk.append(k.reshape(b, l, FOX_HEADS, HEAD_DIM))
            nv.append(v.reshape(b, l, FOX_HEADS, HEAD_DIM))
            nlf.append(lf)
            nsc.append(sc[:, SUBLANES - st_sconv.shape[2]:])
        else:
            w = odd[j]
            x, cs, ss = _odd_call(x, _tail8(st_ssm_conv[j]), None if st_ssm is None else st_ssm[j], mix_gain[i],
                                  w["wz"], w["wx"], w["wdt"], w["cw"], w["cb"], w["dtb"], w["alog"], w["dsk"],
                                  w["nw"], w["wout"], lb)
            nsmc.append(cs[:, SUBLANES - st_ssm_conv.shape[2]:])
            nsm.append(ss)
        w = ffn[i]
        x, fs = _ffn_call(x, _tail8(st_ffn[i]), w["gain"], w["wup"], w["cw"], w["cb"], w["wdn"], lb)
        nff.append(fs[:, SUBLANES - st_ffn.shape[2]:])
    return (x, jnp.stack(nk), jnp.stack(nv), jnp.stack(nlf), jnp.stack(nsc),
            jnp.stack(nsmc), jnp.stack(nsm), jnp.stack(nff))


def kernel(x_prompt, x_sample, cache_fox_k, cache_fox_v, cache_fox_logf, state_sconv, state_ssm_conv, state_ssm, state_ffn_conv, norm_mix, norm_ffn, w_in_even, conv_a_w, q_norm, k_norm, b_forget, w_out_even, w_in_odd, ssm_conv_w, ssm_conv_b, dt_bias, a_log, d_skip, ssm_norm, w_out_odd, w_up, ffn_conv_w, ffn_conv_b, w_down):
    prep = _prepare(dict(
        norm_mix=norm_mix, norm_ffn=norm_ffn, w_in_even=w_in_even, conv_a_w=conv_a_w, q_norm=q_norm,
        k_norm=k_norm, b_forget=b_forget, w_out_even=w_out_even, w_in_odd=w_in_odd, ssm_conv_w=ssm_conv_w,
        ssm_conv_b=ssm_conv_b, dt_bias=dt_bias, a_log=a_log, d_skip=d_skip, ssm_norm=ssm_norm,
        w_out_odd=w_out_odd, w_up=w_up, ffn_conv_w=ffn_conv_w, ffn_conv_b=ffn_conv_b, w_down=w_down))
    bp = x_prompt.shape[0]
    zeros = lambda s: jnp.zeros((s.shape[0], bp) + s.shape[2:], s.dtype)
    p_out = _trunk(x_prompt, None, None, None, zeros(state_sconv), zeros(state_ssm_conv), None,
                   zeros(state_ffn_conv), prep, min(256, x_prompt.shape[1]))
    s_out = _trunk(x_sample, cache_fox_k, cache_fox_v, cache_fox_logf, state_sconv, state_ssm_conv, state_ssm,
                   state_ffn_conv, prep, x_sample.shape[1])
    return (p_out[0], s_out[0]) + p_out[1:] + s_out[1:]
```

```python
import functools
import math

import jax
import jax.numpy as jnp
from jax import lax
from jax.experimental import pallas as pl
from jax.experimental.pallas import tpu as pltpu

F32 = jnp.float32
BF16 = jnp.bfloat16
EPS = 1e-6

LANES = 128
SUBLANES = 8
MXU_DIM = 256

HEAD_DIM = 64
FOX_HEADS = 8
FOX_WIDTH = FOX_HEADS * HEAD_DIM
SC_WIDTH = 512
SSM_HEAD_DIM = 64
SSM_GROUPS = 4
D_STATE = 128
SSD_CHUNK = 128
VMEM_LIMIT = 60 * 1024 * 1024
EVEN_ROWS = 256
ODD_ROWS = 256
FFN_ROWS = 512


def _dot(a, b):
    return jnp.dot(a, b, preferred_element_type=F32)


def _rms(x, gain):
    ms = jnp.mean(x * x, axis=-1, keepdims=True)
    return x * lax.rsqrt(ms + EPS) * gain


def _silu(x):
    return x / (1.0 + jnp.exp(-x))


def _softplus(x):
    return jnp.maximum(x, 0.0) + jnp.log1p(jnp.exp(-jnp.abs(x)))


def _pad_rows(a, rows):
    if a.shape[0] == rows:
        return a
    return jnp.concatenate([a, jnp.zeros((rows - a.shape[0], a.shape[1]), a.dtype)], axis=0)


def _lane_cumsum(x):
    n = x.shape[1]
    lane = lax.broadcasted_iota(jnp.int32, x.shape, 1)
    s = 1
    while s < n:
        x = x + jnp.where(lane >= s, pltpu.roll(x, s, axis=1), 0.0)
        s *= 2
    return x


def _causal_conv(a, tail, w, bias):
    width = w.shape[0]
    row = lax.broadcasted_iota(jnp.int32, a.shape, 0)
    y = a * w[width - 1:width]
    for d in range(1, width):
        sh = pltpu.roll(a, d, axis=0)
        for r in range(d):
            sh = jnp.where(row == r, tail[SUBLANES - d + r:SUBLANES - d + r + 1], sh)
        y = y + sh * w[width - 1 - d:width - d]
    if bias is not None:
        y = y + bias
    return y


def _pair_lane_mask(shape, e):
    lane = lax.broadcasted_iota(jnp.int32, shape, 1)
    return (lane < HEAD_DIM) if e == 0 else (lane >= HEAD_DIM)


def _expand_heads(v, g):
    first_half = _pair_lane_mask((v.shape[0], LANES), 0)
    pieces = []
    for j in range(4):
        h0 = 8 * g + 2 * j
        pieces.append(jnp.where(first_half, v[:, h0:h0 + 1], v[:, h0 + 1:h0 + 2]))
    return jnp.concatenate(pieces, axis=1)


def _ffn_body(x_ref, st_ref, g_ref, wup_ref, cw_ref, cb_ref, wdn_ref, o_ref, nst_ref,
              h_ref, a_ref, gt_ref, y_ref, *, lb, dff, cwid):
    @pl.when(pl.program_id(1) == 0)
    def _():
        nst_ref[...] = st_ref[...]

    x = x_ref[0]
    h_ref[...] = _rms(x, g_ref[...]).astype(BF16)
    n_chunks = dff // cwid

    def up(c):
        lo = c * cwid
        a_ref[c % 2] = _dot(h_ref[...], wup_ref[:, lo:lo + cwid])
        gt_ref[c % 2] = _dot(h_ref[...], wup_ref[:, dff + lo:dff + lo + cwid])

    up(0)
    for c in range(n_chunks):
        lo = c * cwid
        if c + 1 < n_chunks:
            up(c + 1)
        a = a_ref[c % 2]
        y = _causal_conv(a, nst_ref[0, :, lo:lo + cwid], cw_ref[:, lo:lo + cwid], cb_ref[:, lo:lo + cwid])
        nst_ref[0, :, lo:lo + cwid] = a[lb - SUBLANES:lb]
        y_ref[:, lo:lo + cwid] = (_silu(y) * gt_ref[c % 2]).astype(BF16)
    o_ref[0] = x + _dot(y_ref[...], wdn_ref[...])


def _ffn_call(x, st8, gain, wup, cw, cb, wdn, lb):
    b, l, d = x.shape
    dff = wdn.shape[0]
    const = lambda i, j: (0, 0)
    return pl.pallas_call(
        functools.partial(_ffn_body, lb=lb, dff=dff, cwid=MXU_DIM),
        grid=(b, l // lb),
        in_specs=[
            pl.BlockSpec((1, lb, d), lambda i, j: (i, j, 0)),
            pl.BlockSpec((1, SUBLANES, dff), lambda i, j: (i, 0, 0)),
            pl.BlockSpec((1, d), const),
            pl.BlockSpec((d, 2 * dff), const),
            pl.BlockSpec(cw.shape, const),
            pl.BlockSpec((1, dff), const),
            pl.BlockSpec((dff, d), const),
        ],
        out_specs=[
            pl.BlockSpec((1, lb, d), lambda i, j: (i, j, 0)),
            pl.BlockSpec((1, SUBLANES, dff), lambda i, j: (i, 0, 0)),
        ],
        out_shape=[jax.ShapeDtypeStruct(x.shape, F32), jax.ShapeDtypeStruct((b, SUBLANES, dff), F32)],
        scratch_shapes=[pltpu.VMEM((lb, d), BF16),
                        pltpu.VMEM((2, lb, MXU_DIM), F32),
                        pltpu.VMEM((2, lb, MXU_DIM), F32),
                        pltpu.VMEM((lb, dff), BF16)],
        compiler_params=pltpu.CompilerParams(
            dimension_semantics=("arbitrary", "arbitrary"), vmem_limit_bytes=VMEM_LIMIT),
        name="conv_ffn",
    )(x, st8, gain, wup, cw, cb, wdn)


def _head_norm(q, gain):
    r = lax.shift_right_logical(lax.broadcasted_iota(jnp.int32, (MXU_DIM, MXU_DIM), 0), 6)
    c = lax.shift_right_logical(lax.broadcasted_iota(jnp.int32, (MXU_DIM, MXU_DIM), 1), 6)
    gmat = jnp.where(r == c, 1.0 / HEAD_DIM, 0.0).astype(BF16)
    qq = (q * q).astype(BF16)
    ms = jnp.concatenate([_dot(qq[:, i * MXU_DIM:(i + 1) * MXU_DIM], gmat)
                          for i in range(q.shape[1] // MXU_DIM)], axis=1)
    return q * lax.rsqrt(ms + EPS) * gain


def _even_body(*refs, lb, seq, past, tk, wd):
    if past:
        (x_ref, ck_ref, cv_ref, clf_ref, st_ref, g_ref, win_ref, wf_ref, cw_ref, qg_ref, kg_ref, bf_ref,
         wout_ref, xo_ref, ko_ref, vo_ref, lfo_ref, nst_ref,
         k_s, vt_s, cbc_s, ccar_s, qm_s, cq_s, m_s, den_s, acc_s, t_s) = refs
    else:
        (x_ref, st_ref, g_ref, win_ref, wf_ref, cw_ref, qg_ref, kg_ref, bf_ref,
         wout_ref, xo_ref, ko_ref, vo_ref, lfo_ref, nst_ref,
         k_s, vt_s, cbc_s, ccar_s, qm_s, cq_s, m_s, den_s, acc_s, t_s) = refs
    blk = pl.program_id(1)

    @pl.when(blk == 0)
    def _():
        nst_ref[...] = st_ref[...]
        ccar_s[...] = jnp.zeros(ccar_s.shape, F32)
        if past:
            k_s[0:past, :] = ck_ref[0].astype(BF16)
            vt_s[:, 0:past] = cv_ref[0].T.astype(BF16)
            cpast = _lane_cumsum(clf_ref[0])
            ccar_s[0:SUBLANES, :] = jnp.broadcast_to(cpast[:, past - 1:past], (SUBLANES, LANES))
            cum_past = _pad_rows(cpast, LANES).T
            for hh in range(FOX_HEADS):
                cbc_s[hh, 0:past, :] = jnp.broadcast_to(cum_past[:, hh:hh + 1], (past, LANES))

    x = x_ref[0]
    h = _rms(x, g_ref[...]).astype(BF16)
    gate_b, gate_c, u, q, k, v = [_dot(h, win_ref[:, i * SC_WIDTH:(i + 1) * SC_WIDTH]) for i in range(6)]
    f_logit = _dot(h, wf_ref[...])

    cu = gate_c * u
    a_out = gate_b * _causal_conv(cu, nst_ref[0], cw_ref[...], None)
    nst_ref[0] = cu[lb - SUBLANES:lb]

    qn = _head_norm(q, qg_ref[...])
    kn = _head_norm(k, kg_ref[...])
    ko_ref[0] = kn
    vo_ref[0] = v
    lane = lax.broadcasted_iota(jnp.int32, (lb, LANES), 1)
    logf = jnp.where(lane < FOX_HEADS, -_softplus(-(f_logit + bf_ref[...])), 0.0)
    lfo_ref[0] = logf[:, 0:FOX_HEADS]

    pos0 = past if seq == lb else pl.multiple_of(past + blk * lb, LANES)
    cumt_blk = _lane_cumsum(_pad_rows(logf, wd).T) + ccar_s[:, 0:1]
    ccar_s[...] = jnp.broadcast_to(cumt_blk[:, lb - 1:lb], (LANES, LANES))
    cum = cumt_blk.T
    for hh in range(FOX_HEADS):
        cbc_s[hh, pl.ds(pos0, wd), :] = jnp.broadcast_to(cum[:, hh:hh + 1], (wd, LANES))
    k_s[pl.ds(pos0, wd), :] = _pad_rows(kn, wd).astype(BF16)
    vt_s[:, pl.ds(pos0, wd)] = _pad_rows(v, wd).T.astype(BF16)

    qt = _pad_rows(qn * (HEAD_DIM ** -0.5), wd).T
    upper = lax.broadcasted_iota(jnp.int32, (LANES, wd), 0) < HEAD_DIM
    for g in range(FOX_HEADS // 2):
        qt_pair = qt[LANES * g:LANES * (g + 1)]
        qm_s[2 * g] = jnp.where(upper, qt_pair, 0.0).astype(BF16)
        qm_s[2 * g + 1] = jnp.where(upper, 0.0, qt_pair).astype(BF16)
    for hh in range(FOX_HEADS):
        cq_s[hh] = cumt_blk[hh:hh + 1]

    def key_block(off, width, diagonal):
        def stage(hh):
            pair = slice(LANES * (hh // 2), LANES * (hh // 2 + 1))
            kq = _dot(k_s[pl.ds(off, width), pair], qm_s[hh])
            ck = cbc_s[hh, pl.ds(off, width), :]
            t_s[hh % 2, 0:width, :] = kq - jnp.concatenate([ck] * (wd // LANES), axis=1)

        stage(0)
        for hh in range(FOX_HEADS):
            if hh + 1 < FOX_HEADS:
                stage(hh + 1)
            cq = cq_s[hh]
            t = t_s[hh % 2, 0:width, :]
            vt = vt_s[HEAD_DIM * hh:HEAD_DIM * (hh + 1), pl.ds(off, width)]
            if diagonal:
                causal = (lax.broadcasted_iota(jnp.int32, (width, wd), 0)
                          <= lax.broadcasted_iota(jnp.int32, (width, wd), 1))
                t = jnp.where(causal, t, -jnp.inf)
                m = jnp.max(t, axis=0, keepdims=True) + cq
                p = jnp.exp(t + (cq - m))
                den_s[hh] = jnp.sum(p, axis=0, keepdims=True)
                acc_s[hh] = _dot(vt, p.astype(BF16))
            else:
                m_old = m_s[hh]
                m = jnp.maximum(m_old, jnp.max(t, axis=0, keepdims=True) + cq)
                alpha = jnp.exp(m_old - m)
                p = jnp.exp(t + (cq - m))
                den_s[hh] = alpha * den_s[hh] + jnp.sum(p, axis=0, keepdims=True)
                acc_s[hh] = alpha * acc_s[hh] + _dot(vt, p.astype(BF16))
            m_s[hh] = m

    key_block(pos0, wd, True)

    def prev_block(j, carry):
        key_block(pl.multiple_of(j * tk, tk), tk, False)
        return carry

    n_prev = past // tk if seq == lb else past // tk + blk * (lb // tk)
    lax.fori_loop(0, n_prev, prev_block, 0)
    attn_t = jnp.concatenate([acc_s[hh] / den_s[hh] for hh in range(FOX_HEADS)], axis=0)
    attn = attn_t.T[0:lb]

    cat = jnp.concatenate([a_out, attn], axis=1).astype(BF16)
    xo_ref[0] = x + _dot(cat, wout_ref[...])


def _even_call(x, cache, st8, gain, win, wf, cw, qg, kg, bf, wout, lb):
    b, l, d = x.shape
    past = 0 if cache is None else cache[0].shape[1]
    wd = max(lb, LANES)
    tk = lb if past == 0 else min(past, 512)
    assert l % lb == 0 and past % tk == 0 and (l == lb or lb % tk == 0)
    total = past + max(l, wd)
    const = lambda i, j: (0, 0)
    tok = lambda width: pl.BlockSpec((1, lb, width), lambda i, j: (i, j, 0))
    in_specs = [tok(d)]
    args = [x]
    if past:
        ck, cv, clft = cache
        in_specs += [pl.BlockSpec((1, past, FOX_WIDTH), lambda i, j: (i, 0, 0)),
                     pl.BlockSpec((1, past, FOX_WIDTH), lambda i, j: (i, 0, 0)),
                     pl.BlockSpec((1, SUBLANES, past), lambda i, j: (i, 0, 0))]
        args += [ck, cv, clft]
    in_specs += [
        pl.BlockSpec((1, SUBLANES, SC_WIDTH), lambda i, j: (i, 0, 0)),
        pl.BlockSpec((1, d), const),
        pl.BlockSpec(win.shape, const),
        pl.BlockSpec(wf.shape, const),
        pl.BlockSpec(cw.shape, const),
        pl.BlockSpec((1, FOX_WIDTH), const),
        pl.BlockSpec((1, FOX_WIDTH), const),
        pl.BlockSpec((1, LANES), const),
        pl.BlockSpec(wout.shape, const),
    ]
    args += [st8, gain, win, wf, cw, qg, kg, bf, wout]
    return pl.pallas_call(
        functools.partial(_even_body, lb=lb, seq=l, past=past, tk=tk, wd=wd),
        grid=(b, l // lb),
        in_specs=in_specs,
        out_specs=[tok(d), tok(FOX_WIDTH), tok(FOX_WIDTH), tok(FOX_HEADS),
                   pl.BlockSpec((1, SUBLANES, SC_WIDTH), lambda i, j: (i, 0, 0))],
        out_shape=[jax.ShapeDtypeStruct(x.shape, F32),
                   jax.ShapeDtypeStruct((b, l, FOX_WIDTH), F32),
                   jax.ShapeDtypeStruct((b, l, FOX_WIDTH), F32),
                   jax.ShapeDtypeStruct((b, l, FOX_HEADS), F32),
                   jax.ShapeDtypeStruct((b, SUBLANES, SC_WIDTH), F32)],
        scratch_shapes=[pltpu.VMEM((total, FOX_WIDTH), BF16),
                        pltpu.VMEM((FOX_WIDTH, total), BF16),
                        pltpu.VMEM((FOX_HEADS, total, LANES), F32),
                        pltpu.VMEM((LANES, LANES), F32),
                        pltpu.VMEM((FOX_HEADS, LANES, wd), BF16),
                        pltpu.VMEM((FOX_HEADS, 1, wd), F32),
                        pltpu.VMEM((FOX_HEADS, 1, wd), F32),
                        pltpu.VMEM((FOX_HEADS, 1, wd), F32),
                        pltpu.VMEM((FOX_HEADS, HEAD_DIM, wd), F32),
                        pltpu.VMEM((2, max(tk, wd), wd), F32)],
        compiler_params=pltpu.CompilerParams(
            dimension_semantics=("arbitrary", "arbitrary"), vmem_limit_bytes=VMEM_LIMIT),
        name="even_mixer",
    )(*args)


def _ssd_chunk(da, dt, xs, bm, cm, z, s_ref, dsk_ref, nw_ref):
    q = da.shape[0]
    cumt = _lane_cumsum(da.T)
    cum = cumt.T
    dtt = dt.T
    w_end_t = jnp.exp(cumt[:, q - 1:q] - cumt) * dtt
    e_last = jnp.exp(cum[q - 1:q, :])
    tri = (lax.broadcasted_iota(jnp.int32, (q, q), 1) <= lax.broadcasted_iota(jnp.int32, (q, q), 0))
    first_half = _pair_lane_mask((q, LANES), 0)
    first_half_n = _pair_lane_mask((D_STATE, LANES), 0)
    outs = []
    for g in range(SSM_GROUPS):
        b_g = bm[:, D_STATE * g:D_STATE * (g + 1)]
        c_g = cm[:, D_STATE * g:D_STATE * (g + 1)]
        cols = slice(4 * LANES * g, 4 * LANES * (g + 1))
        cb = lax.dot_general(c_g.astype(BF16), b_g.astype(BF16), (((1,), (1,)), ((), ())),
                             preferred_element_type=F32)
        b_t = b_g.T
        pieces = []
        for j in range(4):
            pair = 4 * g + j
            x_pair = xs[:, LANES * pair:LANES * (pair + 1)]
            x_split = jnp.concatenate([jnp.where(first_half, x_pair, 0.0),
                                       jnp.where(first_half, 0.0, x_pair)], axis=0).astype(BF16)
            s_pair = s_ref[pair]
            s_split = jnp.concatenate([jnp.where(first_half_n, s_pair, 0.0),
                                       jnp.where(first_half_n, 0.0, s_pair)], axis=0).astype(BF16)
            mixes, c_scaled, b_scaled = [], [], []
            for e in range(2):
                hh = 2 * pair + e
                cum_col = jnp.broadcast_to(cum[:, hh:hh + 1], (q, LANES))
                decay = jnp.exp(jnp.where(tri, cum_col - cumt[hh:hh + 1, :], -jnp.inf))
                mixes.append((cb * decay * dtt[hh:hh + 1, :]).astype(BF16))
                c_scaled.append((c_g * jnp.exp(cum_col)).astype(BF16))
                b_scaled.append((b_t * w_end_t[hh:hh + 1, :]).astype(BF16))
            pieces.append(_dot(jnp.concatenate(mixes + c_scaled, axis=1),
                               jnp.concatenate([x_split, s_split], axis=0)))
            e_pair = jnp.where(_pair_lane_mask((1, LANES), 0),
                               e_last[:, 2 * pair:2 * pair + 1], e_last[:, 2 * pair + 1:2 * pair + 2])
            s_ref[pair] = s_pair * e_pair + _dot(jnp.concatenate(b_scaled, axis=1), x_split)
        y = jnp.concatenate(pieces, axis=1) + dsk_ref[:, cols] * xs[:, cols]
        y = y * _silu(z[:, cols])
        outs.append((_rms(y, nw_ref[:, cols])).astype(BF16))
    return jnp.concatenate(outs, axis=1)


def _odd_body(*refs, lb, has_state):
    if has_state:
        (x_ref, cst_ref, sst_ref, g_ref, wz_ref, wx_ref, wdt_ref, cw_ref, cb_ref, dtb_ref, alog_ref,
         dsk_ref, nw_ref, wout_ref, xo_ref, ncs_ref, nss_ref,
         s_s, yn_s, z_s, xbc_s, dt_s, da_s, h_s, raw_s) = refs
    else:
        (x_ref, cst_ref, g_ref, wz_ref, wx_ref, wdt_ref, cw_ref, cb_ref, dtb_ref, alog_ref,
         dsk_ref, nw_ref, wout_ref, xo_ref, ncs_ref, nss_ref,
         s_s, yn_s, z_s, xbc_s, dt_s, da_s, h_s, raw_s) = refs
    blk = pl.program_id(1)
    n_pairs = s_s.shape[0]

    @pl.when(blk == 0)
    def _():
        ncs_ref[...] = cst_ref[...]
        for pair in range(n_pairs):
            if has_state:
                s_s[pair] = sst_ref[0, 2 * pair:2 * pair + 2].reshape(2 * SSM_HEAD_DIM, D_STATE).T
            else:
                s_s[pair] = jnp.zeros((D_STATE, 2 * SSM_HEAD_DIM), F32)

    x = x_ref[0]
    h_s[...] = _rms(x, g_ref[...]).astype(BF16)
    cwid = raw_s.shape[2]
    n_chunks = wx_ref.shape[1] // cwid

    def project(c):
        raw_s[c % 2] = _dot(h_s[...], wx_ref[:, cwid * c:cwid * (c + 1)])

    project(0)
    dt = _softplus(_dot(h_s[...], wdt_ref[...]) + dtb_ref[...])
    dt_s[...] = dt
    da_s[...] = dt * (-jnp.exp(alog_ref[...]))
    for c in range(n_chunks):
        cols = slice(cwid * c, cwid * (c + 1))
        if c + 1 < n_chunks:
            project(c + 1)
        else:
            z_s[...] = _dot(h_s[...], wz_ref[...])
        raw = raw_s[c % 2]
        y = _causal_conv(raw, ncs_ref[0, :, cols], cw_ref[:, cols], cb_ref[:, cols])
        ncs_ref[0, :, cols] = raw[lb - SUBLANES:lb]
        xbc_s[:, cols] = _silu(y)

    d_inner = z_s.shape[1]
    bc_w = SSM_GROUPS * D_STATE
    q = SSD_CHUNK
    if lb >= q:
        def chunk(c, carry):
            r = pl.ds(pl.multiple_of(c * q, q), q)
            yn_s[r, :] = _ssd_chunk(da_s[r, :], dt_s[r, :], xbc_s[r, 0:d_inner],
                                    xbc_s[r, d_inner:d_inner + bc_w], xbc_s[r, d_inner + bc_w:d_inner + 2 * bc_w],
                                    z_s[r, :], s_s, dsk_ref, nw_ref)
            return carry
        lax.fori_loop(0, lb // q, chunk, 0)
    else:
        yn = _ssd_chunk(_pad_rows(da_s[...], q), _pad_rows(dt_s[...], q), _pad_rows(xbc_s[:, 0:d_inner], q),
                        _pad_rows(xbc_s[:, d_inner:d_inner + bc_w], q),
                        _pad_rows(xbc_s[:, d_inner + bc_w:d_inner + 2 * bc_w], q),
                        _pad_rows(z_s[...], q), s_s, dsk_ref, nw_ref)
        yn_s[...] = yn[0:lb]
    xo_ref[0] = x + _dot(yn_s[...], wout_ref[...])

    @pl.when(blk == pl.num_programs(1) - 1)
    def _():
        for pair in range(n_pairs):
            nss_ref[0, 2 * pair:2 * pair + 2] = s_s[pair].T.reshape(2, SSM_HEAD_DIM, D_STATE)


def _odd_call(x, cst8, sst, gain, wz, wx, wdt, cw, cb, dtb, alog, dsk, nw, wout, lb):
    b, l, d = x.shape
    d_inner = wz.shape[1]
    conv_dim = wx.shape[1]
    heads = d_inner // SSM_HEAD_DIM
    has_state = sst is not None
    const = lambda i, j: (0, 0)
    in_specs = [pl.BlockSpec((1, lb, d), lambda i, j: (i, j, 0)),
                pl.BlockSpec((1, SUBLANES, conv_dim), lambda i, j: (i, 0, 0))]
    args = [x, cst8]
    if has_state:
        in_specs.append(pl.BlockSpec((1, heads, SSM_HEAD_DIM, D_STATE), lambda i, j: (i, 0, 0, 0)))
        args.append(sst)
    in_specs += [
        pl.BlockSpec((1, d), const),
        pl.BlockSpec(wz.shape, const),
        pl.BlockSpec(wx.shape, const),
        pl.BlockSpec(wdt.shape, const),
        pl.BlockSpec(cw.shape, const),
        pl.BlockSpec((1, conv_dim), const),
        pl.BlockSpec((1, LANES), const),
        pl.BlockSpec((1, LANES), const),
        pl.BlockSpec((1, d_inner), const),
        pl.BlockSpec((1, d_inner), const),
        pl.BlockSpec(wout.shape, const),
    ]
    args += [gain, wz, wx, wdt, cw, cb, dtb, alog, dsk, nw, wout]
    return pl.pallas_call(
        functools.partial(_odd_body, lb=lb, has_state=has_state),
        grid=(b, l // lb),
        in_specs=in_specs,
        out_specs=[pl.BlockSpec((1, lb, d), lambda i, j: (i, j, 0)),
                   pl.BlockSpec((1, SUBLANES, conv_dim), lambda i, j: (i, 0, 0)),
                   pl.BlockSpec((1, heads, SSM_HEAD_DIM, D_STATE), lambda i, j: (i, 0, 0, 0))],
        out_shape=[jax.ShapeDtypeStruct(x.shape, F32),
                   jax.ShapeDtypeStruct((b, SUBLANES, conv_dim), F32),
                   jax.ShapeDtypeStruct((b, heads, SSM_HEAD_DIM, D_STATE), F32)],
        scratch_shapes=[pltpu.VMEM((heads // 2, D_STATE, 2 * SSM_HEAD_DIM), F32),
                        pltpu.VMEM((lb, d_inner), BF16),
                        pltpu.VMEM((lb, d_inner), F32),
                        pltpu.VMEM((lb, conv_dim), F32),
                        pltpu.VMEM((lb, LANES), F32),
                        pltpu.VMEM((lb, LANES), F32),
                        pltpu.VMEM((lb, d), BF16),
                        pltpu.VMEM((2, lb, 4 * LANES), F32)],
        compiler_params=pltpu.CompilerParams(
            dimension_semantics=("arbitrary", "arbitrary"), vmem_limit_bytes=VMEM_LIMIT),
        name="odd_mixer",
    )(*args)


def _tail8(state):
    return jnp.pad(state, ((0, 0), (SUBLANES - state.shape[1], 0), (0, 0)))


def _pad_lanes(a):
    return jnp.pad(a, [(0, 0)] * (a.ndim - 1) + [(0, LANES - a.shape[-1])])


def _prepare(p):
    n_even = p["w_in_even"].shape[0]
    n_odd = p["w_in_odd"].shape[0]
    main = 3 * SC_WIDTH + 3 * FOX_WIDTH
    d_inner = p["w_out_odd"].shape[1]
    conv_dim = p["ssm_conv_w"].shape[2]
    row = lambda a: a.reshape(1, -1)
    even = [dict(
        win=p["w_in_even"][j, :, :main].astype(BF16),
        wf=_pad_lanes(p["w_in_even"][j, :, main:]).astype(BF16),
        cw=p["conv_a_w"][j],
        qg=row(jnp.tile(p["q_norm"][j], FOX_HEADS)),
        kg=row(jnp.tile(p["k_norm"][j], FOX_HEADS)),
        bf=_pad_lanes(row(p["b_forget"][j])),
        wout=p["w_out_even"][j].astype(BF16),
    ) for j in range(n_even)]
    odd = [dict(
        wz=p["w_in_odd"][j, :, :d_inner].astype(BF16),
        wx=p["w_in_odd"][j, :, d_inner:d_inner + conv_dim].astype(BF16),
        wdt=_pad_lanes(p["w_in_odd"][j, :, d_inner + conv_dim:]).astype(BF16),
        cw=p["ssm_conv_w"][j],
        cb=row(p["ssm_conv_b"][j]),
        dtb=_pad_lanes(row(p["dt_bias"][j])),
        alog=_pad_lanes(row(p["a_log"][j])),
        dsk=row(jnp.repeat(p["d_skip"][j], SSM_HEAD_DIM)),
        nw=row(p["ssm_norm"][j]),
        wout=p["w_out_odd"][j].astype(BF16),
    ) for j in range(n_odd)]
    ffn = [dict(
        gain=row(p["norm_ffn"][i]),
        wup=p["w_up"][i].astype(BF16),
        cw=p["ffn_conv_w"][i],
        cb=row(p["ffn_conv_b"][i]),
        wdn=p["w_down"][i].astype(BF16),
    ) for i in range(p["w_up"].shape[0])]
    mix_gain = [row(p["norm_mix"][i]) for i in range(p["norm_mix"].shape[0])]
    return even, odd, ffn, mix_gain


def _block_rows(seq):
    return min(EVEN_ROWS, seq), min(ODD_ROWS, seq), min(FFN_ROWS, seq)


def _trunk(x, cache_k, cache_v, cache_logf, st_sconv, st_ssm_conv, st_ssm, st_ffn, prep):
    even, odd, ffn, mix_gain = prep
    lb_even, lb_odd, lb_ffn = _block_rows(x.shape[1])
    b, l, _ = x.shape
    nk, nv, nlf, nsc, nsmc, nsm, nff = [], [], [], [], [], [], []
    for i in range(len(ffn)):
        j = i // 2
        if i % 2 == 0:
            w = even[j]
            cache = None
            if cache_k is not None:
                clft = jnp.swapaxes(cache_logf[j], 1, 2)
                cache = (cache_k[j].reshape(b, -1, FOX_WIDTH), cache_v[j].reshape(b, -1, FOX_WIDTH), clft)
            x, k, v, lf, sc = _even_call(x, cache, _tail8(st_sconv[j]), mix_gain[i], w["win"], w["wf"], w["cw"],
                                         w["qg"], w["kg"], w["bf"], w["wout"], lb_even)
            nk.append(k.reshape(b, l, FOX_HEADS, HEAD_DIM))
            nv.append(v.reshape(b, l, FOX_HEADS, HEAD_DIM))
            nlf.append(lf)
            nsc.append(sc[:, SUBLANES - st_sconv.shape[2]:])
        else:
            w = odd[j]
            x, cs, ss = _odd_call(x, _tail8(st_ssm_conv[j]), None if st_ssm is None else st_ssm[j], mix_gain[i],
                                  w["wz"], w["wx"], w["wdt"], w["cw"], w["cb"], w["dtb"], w["alog"], w["dsk"],
                                  w["nw"], w["wout"], lb_odd)
            nsmc.append(cs[:, SUBLANES - st_ssm_conv.shape[2]:])
            nsm.append(ss)
        w = ffn[i]
        x, fs = _ffn_call(x, _tail8(st_ffn[i]), w["gain"], w["wup"], w["cw"], w["cb"], w["wdn"], lb_ffn)
        nff.append(fs[:, SUBLANES - st_ffn.shape[2]:])
    return (x, jnp.stack(nk), jnp.stack(nv), jnp.stack(nlf), jnp.stack(nsc),
            jnp.stack(nsmc), jnp.stack(nsm), jnp.stack(nff))


def kernel(x_prompt, x_sample, cache_fox_k, cache_fox_v, cache_fox_logf, state_sconv, state_ssm_conv, state_ssm, state_ffn_conv, norm_mix, norm_ffn, w_in_even, conv_a_w, q_norm, k_norm, b_forget, w_out_even, w_in_odd, ssm_conv_w, ssm_conv_b, dt_bias, a_log, d_skip, ssm_norm, w_out_odd, w_up, ffn_conv_w, ffn_conv_b, w_down):
    prep = _prepare(dict(
        norm_mix=norm_mix, norm_ffn=norm_ffn, w_in_even=w_in_even, conv_a_w=conv_a_w, q_norm=q_norm,
        k_norm=k_norm, b_forget=b_forget, w_out_even=w_out_even, w_in_odd=w_in_odd, ssm_conv_w=ssm_conv_w,
        ssm_conv_b=ssm_conv_b, dt_bias=dt_bias, a_log=a_log, d_skip=d_skip, ssm_norm=ssm_norm,
        w_out_odd=w_out_odd, w_up=w_up, ffn_conv_w=ffn_conv_w, ffn_conv_b=ffn_conv_b, w_down=w_down))
    bp = x_prompt.shape[0]
    zeros = lambda s: jnp.zeros((s.shape[0], bp) + s.shape[2:], s.dtype)
    p_out = _trunk(x_prompt, None, None, None, zeros(state_sconv), zeros(state_ssm_conv), None,
                   zeros(state_ffn_conv), prep)
    s_out = _trunk(x_sample, cache_fox_k, cache_fox_v, cache_fox_logf, state_sconv, state_ssm_conv, state_ssm,
                   state_ffn_conv, prep)
    return (p_out[0], s_out[0]) + p_out[1:] + s_out[1:]
```

```python
import functools
import math

import jax
import jax.numpy as jnp
from jax import lax
from jax.experimental import pallas as pl
from jax.experimental.pallas import tpu as pltpu

F32 = jnp.float32
BF16 = jnp.bfloat16
EPS = 1e-6
LOG2E = math.log2(math.e)

LANES = 128
SUBLANES = 8
MXU_DIM = 256

HEAD_DIM = 64
FOX_HEADS = 8
FOX_WIDTH = FOX_HEADS * HEAD_DIM
SC_WIDTH = 512
SSM_HEAD_DIM = 64
SSM_GROUPS = 4
D_STATE = 128
SSD_CHUNK = 128
VMEM_LIMIT = 60 * 1024 * 1024
EVEN_ROWS = 512
ODD_ROWS = 512
FFN_ROWS = 512


def _resident(shape):
    return pl.BlockSpec(shape, lambda i, j: (0, 0), pipeline_mode=pl.Buffered(1))


def _dot(a, b):
    return jnp.dot(a, b, preferred_element_type=F32)


def _rms(x, gain):
    ms = jnp.mean(x * x, axis=-1, keepdims=True)
    return x * lax.rsqrt(ms + EPS) * gain


def _silu(x):
    return x / (1.0 + jnp.exp(-x))


def _softplus(x):
    return jnp.maximum(x, 0.0) + jnp.log1p(jnp.exp(-jnp.abs(x)))


def _pad_rows(a, rows):
    if a.shape[0] == rows:
        return a
    return jnp.concatenate([a, jnp.zeros((rows - a.shape[0], a.shape[1]), a.dtype)], axis=0)


def _lane_cumsum(x, segment=None):
    n = x.shape[1] if segment is None else segment
    lane = lax.broadcasted_iota(jnp.int32, x.shape, 1)
    if segment is not None:
        lane = lane & (segment - 1)
    s = 1
    while s < n:
        x = x + jnp.where(lane >= s, pltpu.roll(x, s, axis=1), 0.0)
        s *= 2
    return x


def _causal_conv(buf, w, bias):
    width = w.shape[0]
    t = buf.shape[0] - SUBLANES
    y = buf[SUBLANES:SUBLANES + t, :] * w[width - 1:width]
    for d in range(1, width):
        y = y + buf[SUBLANES - d:SUBLANES - d + t, :] * w[width - 1 - d:width - d]
    if bias is not None:
        y = y + bias
    return y


def _row_cumsum(x):
    t = x.shape[0]
    tri = jnp.where(lax.broadcasted_iota(jnp.int32, (t, t), 1) <= lax.broadcasted_iota(jnp.int32, (t, t), 0),
                    1.0, 0.0).astype(BF16)
    hi = x.astype(BF16)
    rest = x - hi.astype(F32)
    mid = rest.astype(BF16)
    lo = (rest - mid.astype(F32)).astype(BF16)
    parts = _dot(tri, jnp.concatenate([hi, mid, lo], axis=1))
    return parts[:, 0:LANES] + parts[:, LANES:2 * LANES] + parts[:, 2 * LANES:3 * LANES]


def _pair_lane_mask(shape, e):
    lane = lax.broadcasted_iota(jnp.int32, shape, 1)
    return (lane < HEAD_DIM) if e == 0 else (lane >= HEAD_DIM)


def _ffn_body(x_ref, st_ref, g_ref, wup_ref, cw_ref, cb_ref, wdn_ref, o_ref, nst_ref,
              h_ref, a_ref, gt_ref, y_ref, *, lb, dff, cwid):
    @pl.when(pl.program_id(1) == 0)
    def _():
        nst_ref[...] = st_ref[...]

    x = x_ref[0]
    h_ref[...] = _rms(x, g_ref[...]).astype(BF16)
    n_chunks = dff // cwid

    def up(c):
        lo = c * cwid
        a_ref[c % 2, 0:SUBLANES, :] = nst_ref[0, :, lo:lo + cwid]
        a_ref[c % 2, SUBLANES:SUBLANES + lb, :] = _dot(h_ref[...], wup_ref[:, lo:lo + cwid])
        gt_ref[c % 2] = _dot(h_ref[...], wup_ref[:, dff + lo:dff + lo + cwid])

    up(0)
    for c in range(n_chunks):
        lo = c * cwid
        if c + 1 < n_chunks:
            up(c + 1)
        y = _causal_conv(a_ref.at[c % 2], cw_ref[:, lo:lo + cwid], cb_ref[:, lo:lo + cwid])
        nst_ref[0, :, lo:lo + cwid] = a_ref[c % 2, lb:lb + SUBLANES, :]
        y_ref[:, lo:lo + cwid] = (_silu(y) * gt_ref[c % 2]).astype(BF16)
    o_ref[0] = x + _dot(y_ref[...], wdn_ref[...])


def _ffn_call(x, st8, gain, wup, cw, cb, wdn, lb):
    b, l, d = x.shape
    dff = wdn.shape[0]
    const = lambda i, j: (0, 0)
    return pl.pallas_call(
        functools.partial(_ffn_body, lb=lb, dff=dff, cwid=MXU_DIM),
        grid=(b, l // lb),
        in_specs=[
            pl.BlockSpec((1, lb, d), lambda i, j: (i, j, 0)),
            pl.BlockSpec((1, SUBLANES, dff), lambda i, j: (i, 0, 0)),
            pl.BlockSpec((1, d), const),
            _resident((d, 2 * dff)),
            pl.BlockSpec(cw.shape, const),
            pl.BlockSpec((1, dff), const),
            _resident((dff, d)),
        ],
        out_specs=[
            pl.BlockSpec((1, lb, d), lambda i, j: (i, j, 0)),
            pl.BlockSpec((1, SUBLANES, dff), lambda i, j: (i, 0, 0)),
        ],
        out_shape=[jax.ShapeDtypeStruct(x.shape, F32), jax.ShapeDtypeStruct((b, SUBLANES, dff), F32)],
        scratch_shapes=[pltpu.VMEM((lb, d), BF16),
                        pltpu.VMEM((2, SUBLANES + lb, MXU_DIM), F32),
                        pltpu.VMEM((2, lb, MXU_DIM), F32),
                        pltpu.VMEM((lb, dff), BF16)],
        compiler_params=pltpu.CompilerParams(
            dimension_semantics=("arbitrary", "arbitrary"), vmem_limit_bytes=VMEM_LIMIT),
        name="conv_ffn",
    )(x, st8, gain, wup, cw, cb, wdn)


def _head_norm(q, gain):
    r = lax.shift_right_logical(lax.broadcasted_iota(jnp.int32, (MXU_DIM, MXU_DIM), 0), 6)
    c = lax.shift_right_logical(lax.broadcasted_iota(jnp.int32, (MXU_DIM, MXU_DIM), 1), 6)
    gmat = jnp.where(r == c, 1.0 / HEAD_DIM, 0.0).astype(BF16)
    qq = (q * q).astype(BF16)
    ms = jnp.concatenate([_dot(qq[:, i * MXU_DIM:(i + 1) * MXU_DIM], gmat)
                          for i in range(q.shape[1] // MXU_DIM)], axis=1)
    return q * lax.rsqrt(ms + EPS) * gain


def _even_body(*refs, lb, seq, past, tk, wd):
    if past:
        (x_ref, ck_ref, cv_ref, clf_ref, st_ref, g_ref, win_ref, wf_ref, cw_ref, qg_ref, kg_ref, bf_ref,
         wout_ref, xo_ref, ko_ref, vo_ref, lfo_ref, nst_ref,
         k_s, vt_s, cbc_s, ccar_s, qm_s, cq_s, m_s, den_s, acc_s, t_s, h_s, proj_s, cu_s, cat_s) = refs
    else:
        (x_ref, st_ref, g_ref, win_ref, wf_ref, cw_ref, qg_ref, kg_ref, bf_ref,
         wout_ref, xo_ref, ko_ref, vo_ref, lfo_ref, nst_ref,
         k_s, vt_s, cbc_s, ccar_s, qm_s, cq_s, m_s, den_s, acc_s, t_s, h_s, proj_s, cu_s, cat_s) = refs
    blk = pl.program_id(1)

    @pl.when(blk == 0)
    def _():
        nst_ref[...] = st_ref[...]
        ccar_s[...] = jnp.zeros(ccar_s.shape, F32)
        if past:
            k_s[0:past, :] = ck_ref[0].astype(BF16)
            vt_s[:, 0:past] = cv_ref[0].T.astype(BF16)
            cpast = _lane_cumsum(clf_ref[0])
            cum_past = _pad_rows(cpast, LANES).T
            ccar_s[...] = jnp.broadcast_to(cum_past[past - 1:past, :], (SUBLANES, LANES))
            cum_past = cum_past * LOG2E
            for hh in range(FOX_HEADS):
                cbc_s[hh, 0:past, :] = jnp.broadcast_to(cum_past[:, hh:hh + 1], (past, LANES))

    x = x_ref[0]
    h_s[...] = _rms(x, g_ref[...]).astype(BF16)
    f_logit = _dot(h_s[...], wf_ref[...])
    for i in (3, 4, 5, 1, 2, 0):
        cols = slice(i * SC_WIDTH, (i + 1) * SC_WIDTH)
        proj_s[:, cols] = _dot(h_s[...], win_ref[:, cols])
    part = lambda i: proj_s[:, i * SC_WIDTH:(i + 1) * SC_WIDTH]

    qn = _head_norm(part(3), qg_ref[...])
    kn = _head_norm(part(4), kg_ref[...])
    v = part(5)
    ko_ref[0] = kn
    vo_ref[0] = v
    lane = lax.broadcasted_iota(jnp.int32, (lb, LANES), 1)
    logf = jnp.where(lane < FOX_HEADS, -_softplus(-(f_logit + bf_ref[...])), 0.0)
    lfo_ref[0] = logf[:, 0:FOX_HEADS]

    pos0 = past if seq == lb else pl.multiple_of(past + blk * lb, LANES)
    cum = _row_cumsum(_pad_rows(logf, wd)) + ccar_s[0:1, :]
    ccar_s[...] = jnp.broadcast_to(cum[lb - 1:lb, :], (SUBLANES, LANES))
    cum = cum * LOG2E
    cumt_blk = cum.T
    for hh in range(FOX_HEADS):
        cbc_s[hh, pl.ds(pos0, wd), :] = jnp.broadcast_to(cum[:, hh:hh + 1], (wd, LANES))
    k_s[pl.ds(pos0, wd), :] = _pad_rows(kn, wd).astype(BF16)
    vt_s[:, pl.ds(pos0, wd)] = _pad_rows(v, wd).T.astype(BF16)

    qt = _pad_rows(qn * (HEAD_DIM ** -0.5 * LOG2E), wd).T
    upper = lax.broadcasted_iota(jnp.int32, (LANES, wd), 0) < HEAD_DIM
    for g in range(FOX_HEADS // 2):
        qt_pair = qt[LANES * g:LANES * (g + 1)]
        qm_s[2 * g] = jnp.where(upper, qt_pair, 0.0).astype(BF16)
        qm_s[2 * g + 1] = jnp.where(upper, 0.0, qt_pair).astype(BF16)
    for hh in range(FOX_HEADS):
        cq_s[hh] = cumt_blk[hh:hh + 1]

    cu_s[0:SUBLANES, :] = nst_ref[0]
    cu_s[SUBLANES:SUBLANES + lb, :] = part(1) * part(2)
    cat_s[:, 0:SC_WIDTH] = (part(0) * _causal_conv(cu_s, cw_ref[...], None)).astype(BF16)
    nst_ref[0] = cu_s[lb:lb + SUBLANES, :]

    def stage_scores(hh, off, width):
        pair = slice(LANES * (hh // 2), LANES * (hh // 2 + 1))
        kq = _dot(k_s[pl.ds(off, width), pair], qm_s[hh])
        ck = cbc_s[hh, pl.ds(off, width), :]
        t_s[hh, 0:width, :] = kq - jnp.concatenate([ck] * (wd // LANES), axis=1)

    def softmax_update(hh, off, width, diagonal):
        cq = cq_s[hh]
        t = t_s[hh, 0:width, :]
        vt = vt_s[HEAD_DIM * hh:HEAD_DIM * (hh + 1), pl.ds(off, width)]
        if diagonal:
            causal = (lax.broadcasted_iota(jnp.int32, (width, wd), 0)
                      <= lax.broadcasted_iota(jnp.int32, (width, wd), 1))
            t = jnp.where(causal, t, -jnp.inf)
            m = jnp.max(t, axis=0, keepdims=True) + cq
            p = jnp.exp2(t + (cq - m))
            den_s[hh] = jnp.sum(p, axis=0, keepdims=True)
            acc_s[hh] = _dot(vt, p.astype(BF16))
        else:
            m_old = m_s[hh]
            m = jnp.maximum(m_old, jnp.max(t, axis=0, keepdims=True) + cq)
            alpha = jnp.exp2(m_old - m)
            p = jnp.exp2(t + (cq - m))
            den_s[hh] = alpha * den_s[hh] + jnp.sum(p, axis=0, keepdims=True)
            acc_s[hh] = alpha * acc_s[hh] + _dot(vt, p.astype(BF16))
        m_s[hh] = m

    n_prev = past // tk if seq == lb else past // tk + blk * (lb // tk)
    for hh in range(FOX_HEADS):
        stage_scores(hh, pos0, wd)
    for hh in range(FOX_HEADS):
        softmax_update(hh, pos0, wd, True)
        stage_scores(hh, 0, tk)

    def prev_block(j, carry):
        off = pl.multiple_of(j * tk, tk)
        nxt = pl.multiple_of(jnp.minimum(j + 1, n_prev - 1) * tk, tk)
        for hh in range(FOX_HEADS):
            softmax_update(hh, off, tk, False)
            stage_scores(hh, nxt, tk)
        return carry

    lax.fori_loop(0, n_prev, prev_block, 0)
    attn_t = jnp.concatenate([acc_s[hh] / den_s[hh] for hh in range(FOX_HEADS)], axis=0)
    attn = attn_t.T[0:lb]

    cat_s[:, SC_WIDTH:SC_WIDTH + FOX_WIDTH] = attn.astype(BF16)
    xo_ref[0] = x + _dot(cat_s[...], wout_ref[...])


def _even_call(x, cache, st8, gain, win, wf, cw, qg, kg, bf, wout, lb):
    b, l, d = x.shape
    past = 0 if cache is None else cache[0].shape[1]
    wd = max(lb, LANES)
    tk = lb if past == 0 else min(past, 512)
    assert l % lb == 0 and past % tk == 0 and (l == lb or lb % tk == 0)
    total = past + max(l, wd)
    const = lambda i, j: (0, 0)
    tok = lambda width: pl.BlockSpec((1, lb, width), lambda i, j: (i, j, 0))
    in_specs = [tok(d)]
    args = [x]
    if past:
        ck, cv, clft = cache
        in_specs += [pl.BlockSpec((1, past, FOX_WIDTH), lambda i, j: (i, 0, 0)),
                     pl.BlockSpec((1, past, FOX_WIDTH), lambda i, j: (i, 0, 0)),
                     pl.BlockSpec((1, SUBLANES, past), lambda i, j: (i, 0, 0))]
        args += [ck, cv, clft]
    in_specs += [
        pl.BlockSpec((1, SUBLANES, SC_WIDTH), lambda i, j: (i, 0, 0)),
        pl.BlockSpec((1, d), const),
        _resident(win.shape),
        pl.BlockSpec(wf.shape, const),
        pl.BlockSpec(cw.shape, const),
        pl.BlockSpec((1, FOX_WIDTH), const),
        pl.BlockSpec((1, FOX_WIDTH), const),
        pl.BlockSpec((1, LANES), const),
        _resident(wout.shape),
    ]
    args += [st8, gain, win, wf, cw, qg, kg, bf, wout]
    return pl.pallas_call(
        functools.partial(_even_body, lb=lb, seq=l, past=past, tk=tk, wd=wd),
        grid=(b, l // lb),
        in_specs=in_specs,
        out_specs=[tok(d), tok(FOX_WIDTH), tok(FOX_WIDTH), tok(FOX_HEADS),
                   pl.BlockSpec((1, SUBLANES, SC_WIDTH), lambda i, j: (i, 0, 0))],
        out_shape=[jax.ShapeDtypeStruct(x.shape, F32),
                   jax.ShapeDtypeStruct((b, l, FOX_WIDTH), F32),
                   jax.ShapeDtypeStruct((b, l, FOX_WIDTH), F32),
                   jax.ShapeDtypeStruct((b, l, FOX_HEADS), F32),
                   jax.ShapeDtypeStruct((b, SUBLANES, SC_WIDTH), F32)],
        scratch_shapes=[pltpu.VMEM((total, FOX_WIDTH), BF16),
                        pltpu.VMEM((FOX_WIDTH, total), BF16),
                        pltpu.VMEM((FOX_HEADS, total, LANES), F32),
                        pltpu.VMEM((SUBLANES, LANES), F32),
                        pltpu.VMEM((FOX_HEADS, LANES, wd), BF16),
                        pltpu.VMEM((FOX_HEADS, 1, wd), F32),
                        pltpu.VMEM((FOX_HEADS, 1, wd), F32),
                        pltpu.VMEM((FOX_HEADS, 1, wd), F32),
                        pltpu.VMEM((FOX_HEADS, HEAD_DIM, wd), F32),
                        pltpu.VMEM((FOX_HEADS, max(tk, wd), wd), F32),
                        pltpu.VMEM((lb, d), BF16),
                        pltpu.VMEM((lb, win.shape[1]), F32),
                        pltpu.VMEM((SUBLANES + lb, SC_WIDTH), F32),
                        pltpu.VMEM((lb, SC_WIDTH + FOX_WIDTH), BF16)],
        compiler_params=pltpu.CompilerParams(
            dimension_semantics=("arbitrary", "arbitrary"), vmem_limit_bytes=VMEM_LIMIT),
        name="even_mixer",
    )(*args)


def _ssd_chunk(cumt, cum, dtt, w_end_t, xs, bm, cm, z, s_ref, dsk_ref, nw_ref):
    q = cum.shape[0]
    e_last = jnp.exp2(cum[q - 1:q, :])
    tri = (lax.broadcasted_iota(jnp.int32, (q, q), 1) <= lax.broadcasted_iota(jnp.int32, (q, q), 0))
    first_half = _pair_lane_mask((q, LANES), 0)
    first_half_n = _pair_lane_mask((D_STATE, LANES), 0)
    outs = []
    for g in range(SSM_GROUPS):
        b_g = bm[:, D_STATE * g:D_STATE * (g + 1)]
        c_g = cm[:, D_STATE * g:D_STATE * (g + 1)]
        cols = slice(4 * LANES * g, 4 * LANES * (g + 1))
        cb = lax.dot_general(c_g.astype(BF16), b_g.astype(BF16), (((1,), (1,)), ((), ())),
                             preferred_element_type=F32)
        b_t = b_g.T
        pieces = []
        for j in range(4):
            pair = 4 * g + j
            x_pair = xs[:, LANES * pair:LANES * (pair + 1)]
            x_split = jnp.concatenate([jnp.where(first_half, x_pair, 0.0),
                                       jnp.where(first_half, 0.0, x_pair)], axis=0).astype(BF16)
            s_pair = s_ref[pair]
            s_split = jnp.concatenate([jnp.where(first_half_n, s_pair, 0.0),
                                       jnp.where(first_half_n, 0.0, s_pair)], axis=0).astype(BF16)
            mixes, c_scaled, b_scaled = [], [], []
            for e in range(2):
                hh = 2 * pair + e
                cum_col = jnp.broadcast_to(cum[:, hh:hh + 1], (q, LANES))
                decay = jnp.exp2(jnp.where(tri, cum_col - cumt[hh:hh + 1, :], -jnp.inf))
                mixes.append((cb * decay * dtt[hh:hh + 1, :]).astype(BF16))
                c_scaled.append((c_g * jnp.exp2(cum_col)).astype(BF16))
                b_scaled.append((b_t * w_end_t[hh:hh + 1, :]).astype(BF16))
            pieces.append(_dot(jnp.concatenate(mixes + c_scaled, axis=1),
                               jnp.concatenate([x_split, s_split], axis=0)))
            e_pair = jnp.where(_pair_lane_mask((1, LANES), 0),
                               e_last[:, 2 * pair:2 * pair + 1], e_last[:, 2 * pair + 1:2 * pair + 2])
            s_ref[pair] = s_pair * e_pair + _dot(jnp.concatenate(b_scaled, axis=1), x_split)
        y = jnp.concatenate(pieces, axis=1) + dsk_ref[:, cols] * xs[:, cols]
        y = y * _silu(z[:, cols])
        outs.append((_rms(y, nw_ref[:, cols])).astype(BF16))
    return jnp.concatenate(outs, axis=1)


def _odd_body(*refs, lb, has_state):
    if has_state:
        (x_ref, cst_ref, sst_ref, g_ref, wz_ref, wx_ref, wdt_ref, cw_ref, cb_ref, dtb_ref, alog_ref,
         dsk_ref, nw_ref, wout_ref, xo_ref, ncs_ref, nss_ref,
         s_s, yn_s, z_s, xbc_s, h_s, raw_s, cumt_s, cum_s, dtt_s, wendt_s) = refs
    else:
        (x_ref, cst_ref, g_ref, wz_ref, wx_ref, wdt_ref, cw_ref, cb_ref, dtb_ref, alog_ref,
         dsk_ref, nw_ref, wout_ref, xo_ref, ncs_ref, nss_ref,
         s_s, yn_s, z_s, xbc_s, h_s, raw_s, cumt_s, cum_s, dtt_s, wendt_s) = refs
    blk = pl.program_id(1)
    n_pairs = s_s.shape[0]

    @pl.when(blk == 0)
    def _():
        ncs_ref[...] = cst_ref[...]
        for pair in range(n_pairs):
            if has_state:
                s_s[pair] = sst_ref[0, 2 * pair:2 * pair + 2].reshape(2 * SSM_HEAD_DIM, D_STATE).T
            else:
                s_s[pair] = jnp.zeros((D_STATE, 2 * SSM_HEAD_DIM), F32)

    x = x_ref[0]
    h_s[...] = _rms(x, g_ref[...]).astype(BF16)
    cwid = raw_s.shape[2]
    n_chunks = wx_ref.shape[1] // cwid

    def project(c):
        cols = slice(cwid * c, cwid * (c + 1))
        raw_s[c % 2, 0:SUBLANES, :] = ncs_ref[0, :, cols]
        raw_s[c % 2, SUBLANES:SUBLANES + lb, :] = _dot(h_s[...], wx_ref[:, cols])

    d_inner = z_s.shape[1]
    z_chunks = d_inner // cwid
    project(0)
    dt = _softplus(_dot(h_s[...], wdt_ref[...]) + dtb_ref[...])
    for c in range(n_chunks):
        cols = slice(cwid * c, cwid * (c + 1))
        if c + 1 < n_chunks:
            project(c + 1)
        if c < z_chunks:
            z_s[:, cols] = _dot(h_s[...], wz_ref[:, cols])
        y = _causal_conv(raw_s.at[c % 2], cw_ref[:, cols], cb_ref[:, cols])
        ncs_ref[0, :, cols] = raw_s[c % 2, lb:lb + SUBLANES, :]
        xbc_s[:, cols] = _silu(y)

    q = SSD_CHUNK
    rows = max(lb, q)
    heads = d_inner // SSM_HEAD_DIM
    dt_t = _pad_rows(dt, rows).T[0:heads]
    da = dt * (-jnp.exp(alog_ref[...]) * LOG2E)
    cum_t = _lane_cumsum(_pad_rows(da, rows).T[0:heads], q)
    c_last = jnp.concatenate([jnp.broadcast_to(cum_t[:, q * (i + 1) - 1:q * (i + 1)], (heads, q))
                              for i in range(rows // q)], axis=1)
    dtt_s[...] = dt_t
    cumt_s[...] = cum_t
    wendt_s[...] = jnp.exp2(c_last - cum_t) * dt_t
    cum_s[...] = _pad_rows(cum_t, LANES).T

    bc_w = SSM_GROUPS * D_STATE
    if lb >= q:
        def chunk(c, carry):
            r = pl.ds(pl.multiple_of(c * q, q), q)
            yn_s[r, :] = _ssd_chunk(cumt_s[:, r], cum_s[r, :], dtt_s[:, r], wendt_s[:, r], xbc_s[r, 0:d_inner],
                                    xbc_s[r, d_inner:d_inner + bc_w], xbc_s[r, d_inner + bc_w:d_inner + 2 * bc_w],
                                    z_s[r, :], s_s, dsk_ref, nw_ref)
            return carry
        lax.fori_loop(0, lb // q, chunk, 0)
    else:
        yn = _ssd_chunk(cumt_s[...], cum_s[...], dtt_s[...], wendt_s[...], _pad_rows(xbc_s[:, 0:d_inner], q),
                        _pad_rows(xbc_s[:, d_inner:d_inner + bc_w], q),
                        _pad_rows(xbc_s[:, d_inner + bc_w:d_inner + 2 * bc_w], q),
                        _pad_rows(z_s[...], q), s_s, dsk_ref, nw_ref)
        yn_s[...] = yn[0:lb]
    xo_ref[0] = x + _dot(yn_s[...], wout_ref[...])

    @pl.when(blk == pl.num_programs(1) - 1)
    def _():
        for pair in range(n_pairs):
            nss_ref[0, 2 * pair:2 * pair + 2] = s_s[pair].T.reshape(2, SSM_HEAD_DIM, D_STATE)


def _odd_call(x, cst8, sst, gain, wz, wx, wdt, cw, cb, dtb, alog, dsk, nw, wout, lb):
    b, l, d = x.shape
    d_inner = wz.shape[1]
    conv_dim = wx.shape[1]
    heads = d_inner // SSM_HEAD_DIM
    ssd_rows = max(lb, SSD_CHUNK)
    has_state = sst is not None
    const = lambda i, j: (0, 0)
    in_specs = [pl.BlockSpec((1, lb, d), lambda i, j: (i, j, 0)),
                pl.BlockSpec((1, SUBLANES, conv_dim), lambda i, j: (i, 0, 0))]
    args = [x, cst8]
    if has_state:
        in_specs.append(pl.BlockSpec((1, heads, SSM_HEAD_DIM, D_STATE), lambda i, j: (i, 0, 0, 0)))
        args.append(sst)
    in_specs += [
        pl.BlockSpec((1, d), const),
        _resident(wz.shape),
        _resident(wx.shape),
        pl.BlockSpec(wdt.shape, const),
        pl.BlockSpec(cw.shape, const),
        pl.BlockSpec((1, conv_dim), const),
        pl.BlockSpec((1, LANES), const),
        pl.BlockSpec((1, LANES), const),
        pl.BlockSpec((1, d_inner), const),
        pl.BlockSpec((1, d_inner), const),
        _resident(wout.shape),
    ]
    args += [gain, wz, wx, wdt, cw, cb, dtb, alog, dsk, nw, wout]
    return pl.pallas_call(
        functools.partial(_odd_body, lb=lb, has_state=has_state),
        grid=(b, l // lb),
        in_specs=in_specs,
        out_specs=[pl.BlockSpec((1, lb, d), lambda i, j: (i, j, 0)),
                   pl.BlockSpec((1, SUBLANES, conv_dim), lambda i, j: (i, 0, 0)),
                   pl.BlockSpec((1, heads, SSM_HEAD_DIM, D_STATE), lambda i, j: (i, 0, 0, 0))],
        out_shape=[jax.ShapeDtypeStruct(x.shape, F32),
                   jax.ShapeDtypeStruct((b, SUBLANES, conv_dim), F32),
                   jax.ShapeDtypeStruct((b, heads, SSM_HEAD_DIM, D_STATE), F32)],
        scratch_shapes=[pltpu.VMEM((heads // 2, D_STATE, 2 * SSM_HEAD_DIM), F32),
                        pltpu.VMEM((lb, d_inner), BF16),
                        pltpu.VMEM((lb, d_inner), F32),
                        pltpu.VMEM((lb, conv_dim), F32),
                        pltpu.VMEM((lb, d), BF16),
                        pltpu.VMEM((2, SUBLANES + lb, 4 * LANES), F32),
                        pltpu.VMEM((heads, ssd_rows), F32),
                        pltpu.VMEM((ssd_rows, LANES), F32),
                        pltpu.VMEM((heads, ssd_rows), F32),
                        pltpu.VMEM((heads, ssd_rows), F32)],
        compiler_params=pltpu.CompilerParams(
            dimension_semantics=("arbitrary", "arbitrary"), vmem_limit_bytes=VMEM_LIMIT),
        name="odd_mixer",
    )(*args)


def _tail8(state):
    return jnp.pad(state, ((0, 0), (SUBLANES - state.shape[1], 0), (0, 0)))


def _pad_lanes(a):
    return jnp.pad(a, [(0, 0)] * (a.ndim - 1) + [(0, LANES - a.shape[-1])])


def _prepare(p):
    n_even = p["w_in_even"].shape[0]
    n_odd = p["w_in_odd"].shape[0]
    main = 3 * SC_WIDTH + 3 * FOX_WIDTH
    d_inner = p["w_out_odd"].shape[1]
    conv_dim = p["ssm_conv_w"].shape[2]
    row = lambda a: a.reshape(1, -1)
    even = [dict(
        win=p["w_in_even"][j, :, :main].astype(BF16),
        wf=_pad_lanes(p["w_in_even"][j, :, main:]).astype(BF16),
        cw=p["conv_a_w"][j],
        qg=row(jnp.tile(p["q_norm"][j], FOX_HEADS)),
        kg=row(jnp.tile(p["k_norm"][j], FOX_HEADS)),
        bf=_pad_lanes(row(p["b_forget"][j])),
        wout=p["w_out_even"][j].astype(BF16),
    ) for j in range(n_even)]
    odd = [dict(
        wz=p["w_in_odd"][j, :, :d_inner].astype(BF16),
        wx=p["w_in_odd"][j, :, d_inner:d_inner + conv_dim].astype(BF16),
        wdt=_pad_lanes(p["w_in_odd"][j, :, d_inner + conv_dim:]).astype(BF16),
        cw=p["ssm_conv_w"][j],
        cb=row(p["ssm_conv_b"][j]),
        dtb=_pad_lanes(row(p["dt_bias"][j])),
        alog=_pad_lanes(row(p["a_log"][j])),
        dsk=row(jnp.repeat(p["d_skip"][j], SSM_HEAD_DIM)),
        nw=row(p["ssm_norm"][j]),
        wout=p["w_out_odd"][j].astype(BF16),
    ) for j in range(n_odd)]
    ffn = [dict(
        gain=row(p["norm_ffn"][i]),
        wup=p["w_up"][i].astype(BF16),
        cw=p["ffn_conv_w"][i],
        cb=row(p["ffn_conv_b"][i]),
        wdn=p["w_down"][i].astype(BF16),
    ) for i in range(p["w_up"].shape[0])]
    mix_gain = [row(p["norm_mix"][i]) for i in range(p["norm_mix"].shape[0])]
    return even, odd, ffn, mix_gain


def _block_rows(seq):
    return min(EVEN_ROWS, seq), min(ODD_ROWS, seq), min(FFN_ROWS, seq)


def _trunk(x, cache_k, cache_v, cache_logf, st_sconv, st_ssm_conv, st_ssm, st_ffn, prep):
    even, odd, ffn, mix_gain = prep
    lb_even, lb_odd, lb_ffn = _block_rows(x.shape[1])
    b, l, _ = x.shape
    nk, nv, nlf, nsc, nsmc, nsm, nff = [], [], [], [], [], [], []
    for i in range(len(ffn)):
        j = i // 2
        if i % 2 == 0:
            w = even[j]
            cache = None
            if cache_k is not None:
                clft = jnp.swapaxes(cache_logf[j], 1, 2)
                cache = (cache_k[j].reshape(b, -1, FOX_WIDTH), cache_v[j].reshape(b, -1, FOX_WIDTH), clft)
            x, k, v, lf, sc = _even_call(x, cache, _tail8(st_sconv[j]), mix_gain[i], w["win"], w["wf"], w["cw"],
                                         w["qg"], w["kg"], w["bf"], w["wout"], lb_even)
            nk.append(k.reshape(b, l, FOX_HEADS, HEAD_DIM))
            nv.append(v.reshape(b, l, FOX_HEADS, HEAD_DIM))
            nlf.append(lf)
            nsc.append(sc[:, SUBLANES - st_sconv.shape[2]:])
        else:
            w = odd[j]
            x, cs, ss = _odd_call(x, _tail8(st_ssm_conv[j]), None if st_ssm is None else st_ssm[j], mix_gain[i],
                                  w["wz"], w["wx"], w["wdt"], w["cw"], w["cb"], w["dtb"], w["alog"], w["dsk"],
                                  w["nw"], w["wout"], lb_odd)
            nsmc.append(cs[:, SUBLANES - st_ssm_conv.shape[2]:])
            nsm.append(ss)
        w = ffn[i]
        x, fs = _ffn_call(x, _tail8(st_ffn[i]), w["gain"], w["wup"], w["cw"], w["cb"], w["wdn"], lb_ffn)
        nff.append(fs[:, SUBLANES - st_ffn.shape[2]:])
    return (x, jnp.stack(nk), jnp.stack(nv), jnp.stack(nlf), jnp.stack(nsc),
            jnp.stack(nsmc), jnp.stack(nsm), jnp.stack(nff))


def kernel(x_prompt, x_sample, cache_fox_k, cache_fox_v, cache_fox_logf, state_sconv, state_ssm_conv, state_ssm, state_ffn_conv, norm_mix, norm_ffn, w_in_even, conv_a_w, q_norm, k_norm, b_forget, w_out_even, w_in_odd, ssm_conv_w, ssm_conv_b, dt_bias, a_log, d_skip, ssm_norm, w_out_odd, w_up, ffn_conv_w, ffn_conv_b, w_down):
    prep = _prepare(dict(
        norm_mix=norm_mix, norm_ffn=norm_ffn, w_in_even=w_in_even, conv_a_w=conv_a_w, q_norm=q_norm,
        k_norm=k_norm, b_forget=b_forget, w_out_even=w_out_even, w_in_odd=w_in_odd, ssm_conv_w=ssm_conv_w,
        ssm_conv_b=ssm_conv_b, dt_bias=dt_bias, a_log=a_log, d_skip=d_skip, ssm_norm=ssm_norm,
        w_out_odd=w_out_odd, w_up=w_up, ffn_conv_w=ffn_conv_w, ffn_conv_b=ffn_conv_b, w_down=w_down))
    bp = x_prompt.shape[0]
    zeros = lambda s: jnp.zeros((s.shape[0], bp) + s.shape[2:], s.dtype)
    p_out = _trunk(x_prompt, None, None, None, zeros(state_sconv), zeros(state_ssm_conv), None,
                   zeros(state_ffn_conv), prep)
    s_out = _trunk(x_sample, cache_fox_k, cache_fox_v, cache_fox_logf, state_sconv, state_ssm_conv, state_ssm,
                   state_ffn_conv, prep)
    return (p_out[0], s_out[0]) + p_out[1:] + s_out[1:]
```

```python
import functools
import math

import jax
import jax.numpy as jnp
from jax import lax
from jax.experimental import pallas as pl
from jax.experimental.pallas import tpu as pltpu

F32 = jnp.float32
BF16 = jnp.bfloat16
EPS = 1e-6
LOG2E = math.log2(math.e)

LANES = 128
SUBLANES = 8
MXU_DIM = 256

HEAD_DIM = 64
FOX_HEADS = 8
FOX_WIDTH = FOX_HEADS * HEAD_DIM
SC_WIDTH = 512
SSM_HEAD_DIM = 64
SSM_GROUPS = 4
D_STATE = 128
SSD_CHUNK = 128
VMEM_LIMIT = 60 * 1024 * 1024
EVEN_ROWS = 512
ODD_ROWS = 512
FFN_ROWS = 512


def _resident(shape):
    return pl.BlockSpec(shape, lambda i, j: (0, 0), pipeline_mode=pl.Buffered(1))


def _dot(a, b):
    return jnp.dot(a, b, preferred_element_type=F32)


def _rms(x, gain):
    ms = jnp.mean(x * x, axis=-1, keepdims=True)
    return x * lax.rsqrt(ms + EPS) * gain


def _silu(x):
    return x / (1.0 + jnp.exp(-x))


def _softplus(x):
    return jnp.maximum(x, 0.0) + jnp.log1p(jnp.exp(-jnp.abs(x)))


def _pad_rows(a, rows):
    if a.shape[0] == rows:
        return a
    return jnp.concatenate([a, jnp.zeros((rows - a.shape[0], a.shape[1]), a.dtype)], axis=0)


def _lane_cumsum(x, segment=None):
    n = x.shape[1] if segment is None else segment
    lane = lax.broadcasted_iota(jnp.int32, x.shape, 1)
    if segment is not None:
        lane = lane & (segment - 1)
    s = 1
    while s < n:
        x = x + jnp.where(lane >= s, pltpu.roll(x, s, axis=1), 0.0)
        s *= 2
    return x


def _causal_conv(buf, w, bias):
    width = w.shape[0]
    t = buf.shape[0] - SUBLANES
    y = buf[SUBLANES:SUBLANES + t, :] * w[width - 1:width]
    for d in range(1, width):
        y = y + buf[SUBLANES - d:SUBLANES - d + t, :] * w[width - 1 - d:width - d]
    if bias is not None:
        y = y + bias
    return y


def _row_cumsum(x):
    t = x.shape[0]
    tri = jnp.where(lax.broadcasted_iota(jnp.int32, (t, t), 1) <= lax.broadcasted_iota(jnp.int32, (t, t), 0),
                    1.0, 0.0).astype(BF16)
    hi = x.astype(BF16)
    rest = x - hi.astype(F32)
    mid = rest.astype(BF16)
    lo = (rest - mid.astype(F32)).astype(BF16)
    parts = _dot(tri, jnp.concatenate([hi, mid, lo], axis=1))
    return parts[:, 0:LANES] + parts[:, LANES:2 * LANES] + parts[:, 2 * LANES:3 * LANES]


def _pair_lane_mask(shape, e):
    lane = lax.broadcasted_iota(jnp.int32, shape, 1)
    return (lane < HEAD_DIM) if e == 0 else (lane >= HEAD_DIM)


def _ffn_body(x_ref, st_ref, g_ref, wup_ref, cw_ref, cb_ref, wdn_ref, o_ref, nst_ref,
              h_ref, a_ref, gt_ref, y_ref, *, lb, dff, cwid):
    @pl.when(pl.program_id(1) == 0)
    def _():
        nst_ref[...] = st_ref[...]

    x = x_ref[0]
    h_ref[...] = _rms(x, g_ref[...]).astype(BF16)
    n_chunks = dff // cwid

    def up(c):
        lo = c * cwid
        a_ref[c % 2, 0:SUBLANES, :] = nst_ref[0, :, lo:lo + cwid]
        a_ref[c % 2, SUBLANES:SUBLANES + lb, :] = _dot(h_ref[...], wup_ref[:, lo:lo + cwid])
        gt_ref[c % 2] = _dot(h_ref[...], wup_ref[:, dff + lo:dff + lo + cwid])

    up(0)
    for c in range(n_chunks):
        lo = c * cwid
        if c + 1 < n_chunks:
            up(c + 1)
        y = _causal_conv(a_ref.at[c % 2], cw_ref[:, lo:lo + cwid], cb_ref[:, lo:lo + cwid])
        nst_ref[0, :, lo:lo + cwid] = a_ref[c % 2, lb:lb + SUBLANES, :]
        y_ref[:, lo:lo + cwid] = (_silu(y) * gt_ref[c % 2]).astype(BF16)
    o_ref[0] = x + _dot(y_ref[...], wdn_ref[...])


def _ffn_call(x, st8, gain, wup, cw, cb, wdn, lb):
    b, l, d = x.shape
    dff = wdn.shape[0]
    const = lambda i, j: (0, 0)
    return pl.pallas_call(
        functools.partial(_ffn_body, lb=lb, dff=dff, cwid=MXU_DIM),
        grid=(b, l // lb),
        in_specs=[
            pl.BlockSpec((1, lb, d), lambda i, j: (i, j, 0)),
            pl.BlockSpec((1, SUBLANES, dff), lambda i, j: (i, 0, 0)),
            pl.BlockSpec((1, d), const),
            _resident((d, 2 * dff)),
            pl.BlockSpec(cw.shape, const),
            pl.BlockSpec((1, dff), const),
            _resident((dff, d)),
        ],
        out_specs=[
            pl.BlockSpec((1, lb, d), lambda i, j: (i, j, 0)),
            pl.BlockSpec((1, SUBLANES, dff), lambda i, j: (i, 0, 0)),
        ],
        out_shape=[jax.ShapeDtypeStruct(x.shape, F32), jax.ShapeDtypeStruct((b, SUBLANES, dff), F32)],
        scratch_shapes=[pltpu.VMEM((lb, d), BF16),
                        pltpu.VMEM((2, SUBLANES + lb, MXU_DIM), F32),
                        pltpu.VMEM((2, lb, MXU_DIM), F32),
                        pltpu.VMEM((lb, dff), BF16)],
        compiler_params=pltpu.CompilerParams(
            dimension_semantics=("arbitrary", "arbitrary"), vmem_limit_bytes=VMEM_LIMIT),
        name="conv_ffn",
    )(x, st8, gain, wup, cw, cb, wdn)


def _head_norm(q, gain):
    r = lax.shift_right_logical(lax.broadcasted_iota(jnp.int32, (MXU_DIM, MXU_DIM), 0), 6)
    c = lax.shift_right_logical(lax.broadcasted_iota(jnp.int32, (MXU_DIM, MXU_DIM), 1), 6)
    gmat = jnp.where(r == c, 1.0 / HEAD_DIM, 0.0).astype(BF16)
    qq = (q * q).astype(BF16)
    ms = jnp.concatenate([_dot(qq[:, i * MXU_DIM:(i + 1) * MXU_DIM], gmat)
                          for i in range(q.shape[1] // MXU_DIM)], axis=1)
    return q * lax.rsqrt(ms + EPS) * gain


def _even_body(*refs, lb, seq, past, tk, wd, n_aliased):
    refs = list(refs)
    x_ref = refs.pop(0)
    if past:
        ck_ref, cv_ref, clf_ref = refs[:3]
        del refs[:3]
    st_ref, g_ref, win_ref, wf_ref, cw_ref, qg_ref, kg_ref, bf_ref, wout_ref = refs[:9]
    del refs[:9 + n_aliased]
    (xo_ref, ko_ref, vo_ref, lfo_ref, nst_ref,
     k_s, vt_s, cbc_s, ccar_s, qm_s, cq_s, m_s, den_s, acc_s, t_s, h_s, proj_s, cu_s, cat_s) = refs
    blk = pl.program_id(1)

    @pl.when(blk == 0)
    def _():
        nst_ref[...] = st_ref[...]
        ccar_s[...] = jnp.zeros(ccar_s.shape, F32)
        if past:
            k_s[0:past, :] = ck_ref[0].astype(BF16)
            vt_s[:, 0:past] = cv_ref[0].T.astype(BF16)
            cpast = _lane_cumsum(clf_ref[0])
            cum_past = _pad_rows(cpast, LANES).T
            ccar_s[...] = jnp.broadcast_to(cum_past[past - 1:past, :], (SUBLANES, LANES))
            cum_past = cum_past * LOG2E
            for hh in range(FOX_HEADS):
                cbc_s[hh, 0:past, :] = jnp.broadcast_to(cum_past[:, hh:hh + 1], (past, LANES))

    x = x_ref[0]
    h_s[...] = _rms(x, g_ref[...]).astype(BF16)
    f_logit = _dot(h_s[...], wf_ref[...])
    for i in (3, 4, 5, 1, 2, 0):
        cols = slice(i * SC_WIDTH, (i + 1) * SC_WIDTH)
        proj_s[:, cols] = _dot(h_s[...], win_ref[:, cols])
    part = lambda i: proj_s[:, i * SC_WIDTH:(i + 1) * SC_WIDTH]

    qn = _head_norm(part(3), qg_ref[...])
    kn = _head_norm(part(4), kg_ref[...])
    v = part(5)
    ko_ref[0, 0] = kn
    vo_ref[0, 0] = v
    lane = lax.broadcasted_iota(jnp.int32, (lb, LANES), 1)
    logf = jnp.where(lane < FOX_HEADS, -_softplus(-(f_logit + bf_ref[...])), 0.0)
    lfo_ref[0] = logf[:, 0:FOX_HEADS]

    pos0 = past if seq == lb else pl.multiple_of(past + blk * lb, LANES)
    cum = _row_cumsum(_pad_rows(logf, wd)) + ccar_s[0:1, :]
    ccar_s[...] = jnp.broadcast_to(cum[lb - 1:lb, :], (SUBLANES, LANES))
    cum = cum * LOG2E
    cumt_blk = cum.T
    for hh in range(FOX_HEADS):
        cbc_s[hh, pl.ds(pos0, wd), :] = jnp.broadcast_to(cum[:, hh:hh + 1], (wd, LANES))
    k_s[pl.ds(pos0, wd), :] = _pad_rows(kn, wd).astype(BF16)
    vt_s[:, pl.ds(pos0, wd)] = _pad_rows(v, wd).T.astype(BF16)

    qt = _pad_rows(qn * (HEAD_DIM ** -0.5 * LOG2E), wd).T
    upper = lax.broadcasted_iota(jnp.int32, (LANES, wd), 0) < HEAD_DIM
    for g in range(FOX_HEADS // 2):
        qt_pair = qt[LANES * g:LANES * (g + 1)]
        qm_s[2 * g] = jnp.where(upper, qt_pair, 0.0).astype(BF16)
        qm_s[2 * g + 1] = jnp.where(upper, 0.0, qt_pair).astype(BF16)
    for hh in range(FOX_HEADS):
        cq_s[hh] = cumt_blk[hh:hh + 1]

    cu_s[0:SUBLANES, :] = nst_ref[0]
    cu_s[SUBLANES:SUBLANES + lb, :] = part(1) * part(2)
    cat_s[:, 0:SC_WIDTH] = (part(0) * _causal_conv(cu_s, cw_ref[...], None)).astype(BF16)
    nst_ref[0] = cu_s[lb:lb + SUBLANES, :]

    def stage_scores(hh, off, width):
        pair = slice(LANES * (hh // 2), LANES * (hh // 2 + 1))
        kq = _dot(k_s[pl.ds(off, width), pair], qm_s[hh])
        ck = cbc_s[hh, pl.ds(off, width), :]
        t_s[hh, 0:width, :] = kq - jnp.concatenate([ck] * (wd // LANES), axis=1)

    def softmax_update(hh, off, width, diagonal):
        cq = cq_s[hh]
        t = t_s[hh, 0:width, :]
        vt = vt_s[HEAD_DIM * hh:HEAD_DIM * (hh + 1), pl.ds(off, width)]
        if diagonal:
            causal = (lax.broadcasted_iota(jnp.int32, (width, wd), 0)
                      <= lax.broadcasted_iota(jnp.int32, (width, wd), 1))
            t = jnp.where(causal, t, -jnp.inf)
            m = jnp.max(t, axis=0, keepdims=True) + cq
            p = jnp.exp2(t + (cq - m))
            den_s[hh] = jnp.sum(p, axis=0, keepdims=True)
            acc_s[hh] = _dot(vt, p.astype(BF16))
        else:
            m_old = m_s[hh]
            m = jnp.maximum(m_old, jnp.max(t, axis=0, keepdims=True) + cq)
            alpha = jnp.exp2(m_old - m)
            p = jnp.exp2(t + (cq - m))
            den_s[hh] = alpha * den_s[hh] + jnp.sum(p, axis=0, keepdims=True)
            acc_s[hh] = alpha * acc_s[hh] + _dot(vt, p.astype(BF16))
        m_s[hh] = m

    n_prev = past // tk if seq == lb else past // tk + blk * (lb // tk)
    for hh in range(FOX_HEADS):
        stage_scores(hh, pos0, wd)
    for hh in range(FOX_HEADS):
        softmax_update(hh, pos0, wd, True)
        stage_scores(hh, 0, tk)

    def prev_block(j, carry):
        off = pl.multiple_of(j * tk, tk)
        nxt = pl.multiple_of(jnp.minimum(j + 1, n_prev - 1) * tk, tk)
        for hh in range(FOX_HEADS):
            softmax_update(hh, off, tk, False)
            stage_scores(hh, nxt, tk)
        return carry

    lax.fori_loop(0, n_prev, prev_block, 0)
    attn_t = jnp.concatenate([acc_s[hh] / den_s[hh] for hh in range(FOX_HEADS)], axis=0)
    attn = attn_t.T[0:lb]

    cat_s[:, SC_WIDTH:SC_WIDTH + FOX_WIDTH] = attn.astype(BF16)
    xo_ref[0] = x + _dot(cat_s[...], wout_ref[...])


def _even_call(x, cache, st8, gain, win, wf, cw, qg, kg, bf, wout, lb, layer, n_layers, kv_stack):
    b, l, d = x.shape
    past = 0 if cache is None else cache[0].shape[1]
    wd = max(lb, LANES)
    tk = lb if past == 0 else min(past, 512)
    assert l % lb == 0 and past % tk == 0 and (l == lb or lb % tk == 0)
    total = past + max(l, wd)
    const = lambda i, j: (0, 0)
    tok = lambda width: pl.BlockSpec((1, lb, width), lambda i, j: (i, j, 0))
    in_specs = [tok(d)]
    args = [x]
    if past:
        ck, cv, clft = cache
        in_specs += [pl.BlockSpec((1, past, FOX_WIDTH), lambda i, j: (i, 0, 0)),
                     pl.BlockSpec((1, past, FOX_WIDTH), lambda i, j: (i, 0, 0)),
                     pl.BlockSpec((1, SUBLANES, past), lambda i, j: (i, 0, 0))]
        args += [ck, cv, clft]
    in_specs += [
        pl.BlockSpec((1, SUBLANES, SC_WIDTH), lambda i, j: (i, 0, 0)),
        pl.BlockSpec((1, d), const),
        _resident(win.shape),
        pl.BlockSpec(wf.shape, const),
        pl.BlockSpec(cw.shape, const),
        pl.BlockSpec((1, FOX_WIDTH), const),
        pl.BlockSpec((1, FOX_WIDTH), const),
        pl.BlockSpec((1, LANES), const),
        _resident(wout.shape),
    ]
    args += [st8, gain, win, wf, cw, qg, kg, bf, wout]
    aliases = {}
    if kv_stack is not None:
        aliases = {len(args): 1, len(args) + 1: 2}
        in_specs += [pl.BlockSpec(memory_space=pl.ANY)] * 2
        args += list(kv_stack)
    kv_spec = pl.BlockSpec((1, 1, lb, FOX_WIDTH), lambda i, j: (layer, i, j, 0))
    return pl.pallas_call(
        functools.partial(_even_body, lb=lb, seq=l, past=past, tk=tk, wd=wd, n_aliased=len(aliases)),
        grid=(b, l // lb),
        in_specs=in_specs,
        out_specs=[tok(d), kv_spec, kv_spec, tok(FOX_HEADS),
                   pl.BlockSpec((1, SUBLANES, SC_WIDTH), lambda i, j: (i, 0, 0))],
        out_shape=[jax.ShapeDtypeStruct(x.shape, F32),
                   jax.ShapeDtypeStruct((n_layers, b, l, FOX_WIDTH), F32),
                   jax.ShapeDtypeStruct((n_layers, b, l, FOX_WIDTH), F32),
                   jax.ShapeDtypeStruct((b, l, FOX_HEADS), F32),
                   jax.ShapeDtypeStruct((b, SUBLANES, SC_WIDTH), F32)],
        input_output_aliases=aliases,
        scratch_shapes=[pltpu.VMEM((total, FOX_WIDTH), BF16),
                        pltpu.VMEM((FOX_WIDTH, total), BF16),
                        pltpu.VMEM((FOX_HEADS, total, LANES), F32),
                        pltpu.VMEM((SUBLANES, LANES), F32),
                        pltpu.VMEM((FOX_HEADS, LANES, wd), BF16),
                        pltpu.VMEM((FOX_HEADS, 1, wd), F32),
                        pltpu.VMEM((FOX_HEADS, 1, wd), F32),
                        pltpu.VMEM((FOX_HEADS, 1, wd), F32),
                        pltpu.VMEM((FOX_HEADS, HEAD_DIM, wd), F32),
                        pltpu.VMEM((FOX_HEADS, max(tk, wd), wd), F32),
                        pltpu.VMEM((lb, d), BF16),
                        pltpu.VMEM((lb, win.shape[1]), F32),
                        pltpu.VMEM((SUBLANES + lb, SC_WIDTH), F32),
                        pltpu.VMEM((lb, SC_WIDTH + FOX_WIDTH), BF16)],
        compiler_params=pltpu.CompilerParams(
            dimension_semantics=("arbitrary", "arbitrary"), vmem_limit_bytes=VMEM_LIMIT),
        name="even_mixer",
    )(*args)


def _ssd_chunk(cumt, cum, dtt, w_end_t, xs_at, bm_at, cm_at, z_at, emit, s_ref, dsk_ref, nw_ref,
               between_groups=None):
    q = cum.shape[0]
    e_last = jnp.exp2(cum[q - 1:q, :])
    tri = (lax.broadcasted_iota(jnp.int32, (q, q), 1) <= lax.broadcasted_iota(jnp.int32, (q, q), 0))
    first_half = _pair_lane_mask((q, LANES), 0)
    first_half_n = _pair_lane_mask((D_STATE, LANES), 0)
    for g in range(SSM_GROUPS):
        n_cols = slice(D_STATE * g, D_STATE * (g + 1))
        cols = slice(4 * LANES * g, 4 * LANES * (g + 1))
        b_g = bm_at(n_cols)
        c_g = cm_at(n_cols)
        cb = lax.dot_general(c_g.astype(BF16), b_g.astype(BF16), (((1,), (1,)), ((), ())),
                             preferred_element_type=F32)
        b_t = b_g.T
        pieces = []
        for j in range(4):
            pair = 4 * g + j
            x_pair = xs_at(slice(LANES * pair, LANES * (pair + 1)))
            x_split = jnp.concatenate([jnp.where(first_half, x_pair, 0.0),
                                       jnp.where(first_half, 0.0, x_pair)], axis=0).astype(BF16)
            s_pair = s_ref[pair]
            s_split = jnp.concatenate([jnp.where(first_half_n, s_pair, 0.0),
                                       jnp.where(first_half_n, 0.0, s_pair)], axis=0).astype(BF16)
            mixes, c_scaled, b_scaled = [], [], []
            for e in range(2):
                hh = 2 * pair + e
                cum_col = jnp.broadcast_to(cum[:, hh:hh + 1], (q, LANES))
                decay = jnp.exp2(jnp.where(tri, cum_col - cumt[hh:hh + 1, :], -jnp.inf))
                mixes.append((cb * decay * dtt[hh:hh + 1, :]).astype(BF16))
                c_scaled.append((c_g * jnp.exp2(cum_col)).astype(BF16))
                b_scaled.append((b_t * w_end_t[hh:hh + 1, :]).astype(BF16))
            pieces.append(_dot(jnp.concatenate(mixes + c_scaled, axis=1),
                               jnp.concatenate([x_split, s_split], axis=0)))
            e_pair = jnp.where(_pair_lane_mask((1, LANES), 0),
                               e_last[:, 2 * pair:2 * pair + 1], e_last[:, 2 * pair + 1:2 * pair + 2])
            s_ref[pair] = s_pair * e_pair + _dot(jnp.concatenate(b_scaled, axis=1), x_split)
        y = jnp.concatenate(pieces, axis=1) + dsk_ref[:, cols] * xs_at(cols)
        y = y * _silu(z_at(cols))
        emit(cols, (_rms(y, nw_ref[:, cols])).astype(BF16))
        if between_groups is not None:
            between_groups(g)


def _odd_body(*refs, lb, has_state):
    if has_state:
        (x_ref, cst_ref, sst_ref, g_ref, wz_ref, wx_ref, wdt_ref, cw_ref, cb_ref, dtb_ref, alog_ref,
         dsk_ref, nw_ref, wout_ref, xo_ref, ncs_ref, nss_ref,
         s_s, yn_s, z_s, xbc_s, h_s, raw_s, cumt_s, cum_s, dtt_s, wendt_s) = refs
    else:
        (x_ref, cst_ref, g_ref, wz_ref, wx_ref, wdt_ref, cw_ref, cb_ref, dtb_ref, alog_ref,
         dsk_ref, nw_ref, wout_ref, xo_ref, ncs_ref, nss_ref,
         s_s, yn_s, z_s, xbc_s, h_s, raw_s, cumt_s, cum_s, dtt_s, wendt_s) = refs
    blk = pl.program_id(1)
    n_pairs = s_s.shape[0]

    @pl.when(blk == 0)
    def _():
        ncs_ref[...] = cst_ref[...]
        for pair in range(n_pairs):
            if has_state:
                s_s[pair] = sst_ref[0, 2 * pair:2 * pair + 2].reshape(2 * SSM_HEAD_DIM, D_STATE).T
            else:
                s_s[pair] = jnp.zeros((D_STATE, 2 * SSM_HEAD_DIM), F32)

    x = x_ref[0]
    h_s[...] = _rms(x, g_ref[...]).astype(BF16)
    cwid = raw_s.shape[2]
    n_chunks = wx_ref.shape[1] // cwid

    def project(c):
        cols = slice(cwid * c, cwid * (c + 1))
        raw_s[c % 2, 0:SUBLANES, :] = ncs_ref[0, :, cols]
        raw_s[c % 2, SUBLANES:SUBLANES + lb, :] = _dot(h_s[...], wx_ref[:, cols])

    d_inner = z_s.shape[1]
    z_chunks = d_inner // cwid
    project(0)
    dt = _softplus(_dot(h_s[...], wdt_ref[...]) + dtb_ref[...])
    for c in range(n_chunks):
        cols = slice(cwid * c, cwid * (c + 1))
        if c + 1 < n_chunks:
            project(c + 1)
        if c < z_chunks:
            z_s[:, cols] = _dot(h_s[...], wz_ref[:, cols])
        y = _causal_conv(raw_s.at[c % 2], cw_ref[:, cols], cb_ref[:, cols])
        ncs_ref[0, :, cols] = raw_s[c % 2, lb:lb + SUBLANES, :]
        xbc_s[:, cols] = _silu(y)

    q = SSD_CHUNK
    rows = max(lb, q)
    heads = d_inner // SSM_HEAD_DIM
    dt_t = _pad_rows(dt, rows).T[0:heads]
    da = dt * (-jnp.exp(alog_ref[...]) * LOG2E)
    cum_t = _lane_cumsum(_pad_rows(da, rows).T[0:heads], q)
    c_last = jnp.concatenate([jnp.broadcast_to(cum_t[:, q * (i + 1) - 1:q * (i + 1)], (heads, q))
                              for i in range(rows // q)], axis=1)
    dtt_s[...] = dt_t
    cumt_s[...] = cum_t
    wendt_s[...] = jnp.exp2(c_last - cum_t) * dt_t
    cum_s[...] = _pad_rows(cum_t, LANES).T

    bc_w = SSM_GROUPS * D_STATE
    if lb >= q:
        d_model = x_ref.shape[2]

        def project_out(c, part):
            r = pl.ds(pl.multiple_of(c * q, q), q)
            cols = slice(part * d_model // SSM_GROUPS, (part + 1) * d_model // SSM_GROUPS)
            xo_ref[0, r, cols] = x_ref[0, r, cols] + _dot(yn_s[r, :], wout_ref[:, cols])

        def ssd(c, between_groups=None):
            r = pl.ds(pl.multiple_of(c * q, q), q)

            outs = []
            _ssd_chunk(cumt_s[:, r], cum_s[r, :], dtt_s[:, r], wendt_s[:, r],
                       lambda cols: xbc_s[r, cols],
                       lambda cols: xbc_s[r, slice(d_inner + cols.start, d_inner + cols.stop)],
                       lambda cols: xbc_s[r, slice(d_inner + bc_w + cols.start, d_inner + bc_w + cols.stop)],
                       lambda cols: z_s[r, cols], lambda cols, value: outs.append(value),
                       s_s, dsk_ref, nw_ref, between_groups)
            yn_s[r, :] = jnp.concatenate(outs, axis=1)

        def chunk(c, carry):
            ssd(c, lambda g: project_out(c - 1, g))
            return carry

        ssd(0)
        lax.fori_loop(1, lb // q, chunk, 0)
        for part in range(SSM_GROUPS):
            project_out(lb // q - 1, part)
    else:
        def emit(cols, value):
            yn_s[:, cols] = value[0:lb]

        _ssd_chunk(cumt_s[...], cum_s[...], dtt_s[...], wendt_s[...],
                   lambda cols: _pad_rows(xbc_s[:, cols], q),
                   lambda cols: _pad_rows(xbc_s[:, slice(d_inner + cols.start, d_inner + cols.stop)], q),
                   lambda cols: _pad_rows(xbc_s[:, slice(d_inner + bc_w + cols.start, d_inner + bc_w + cols.stop)], q),
                   lambda cols: _pad_rows(z_s[:, cols], q), emit, s_s, dsk_ref, nw_ref)
        xo_ref[0] = x + _dot(yn_s[...], wout_ref[...])

    @pl.when(blk == pl.num_programs(1) - 1)
    def _():
        for pair in range(n_pairs):
            nss_ref[0, 2 * pair:2 * pair + 2] = s_s[pair].T.reshape(2, SSM_HEAD_DIM, D_STATE)


def _odd_call(x, cst8, sst, gain, wz, wx, wdt, cw, cb, dtb, alog, dsk, nw, wout, lb):
    b, l, d = x.shape
    d_inner = wz.shape[1]
    conv_dim = wx.shape[1]
    heads = d_inner // SSM_HEAD_DIM
    ssd_rows = max(lb, SSD_CHUNK)
    has_state = sst is not None
    const = lambda i, j: (0, 0)
    in_specs = [pl.BlockSpec((1, lb, d), lambda i, j: (i, j, 0)),
                pl.BlockSpec((1, SUBLANES, conv_dim), lambda i, j: (i, 0, 0))]
    args = [x, cst8]
    if has_state:
        in_specs.append(pl.BlockSpec((1, heads, SSM_HEAD_DIM, D_STATE), lambda i, j: (i, 0, 0, 0)))
        args.append(sst)
    in_specs += [
        pl.BlockSpec((1, d), const),
        _resident(wz.shape),
        _resident(wx.shape),
        pl.BlockSpec(wdt.shape, const),
        pl.BlockSpec(cw.shape, const),
        pl.BlockSpec((1, conv_dim), const),
        pl.BlockSpec((1, LANES), const),
        pl.BlockSpec((1, LANES), const),
        pl.BlockSpec((1, d_inner), const),
        pl.BlockSpec((1, d_inner), const),
        _resident(wout.shape),
    ]
    args += [gain, wz, wx, wdt, cw, cb, dtb, alog, dsk, nw, wout]
    return pl.pallas_call(
        functools.partial(_odd_body, lb=lb, has_state=has_state),
        grid=(b, l // lb),
        in_specs=in_specs,
        out_specs=[pl.BlockSpec((1, lb, d), lambda i, j: (i, j, 0)),
                   pl.BlockSpec((1, SUBLANES, conv_dim), lambda i, j: (i, 0, 0)),
                   pl.BlockSpec((1, heads, SSM_HEAD_DIM, D_STATE), lambda i, j: (i, 0, 0, 0))],
        out_shape=[jax.ShapeDtypeStruct(x.shape, F32),
                   jax.ShapeDtypeStruct((b, SUBLANES, conv_dim), F32),
                   jax.ShapeDtypeStruct((b, heads, SSM_HEAD_DIM, D_STATE), F32)],
        scratch_shapes=[pltpu.VMEM((heads // 2, D_STATE, 2 * SSM_HEAD_DIM), F32),
                        pltpu.VMEM((lb, d_inner), BF16),
                        pltpu.VMEM((lb, d_inner), F32),
                        pltpu.VMEM((lb, conv_dim), F32),
                        pltpu.VMEM((lb, d), BF16),
                        pltpu.VMEM((2, SUBLANES + lb, 4 * LANES), F32),
                        pltpu.VMEM((heads, ssd_rows), F32),
                        pltpu.VMEM((ssd_rows, LANES), F32),
                        pltpu.VMEM((heads, ssd_rows), F32),
                        pltpu.VMEM((heads, ssd_rows), F32)],
        compiler_params=pltpu.CompilerParams(
            dimension_semantics=("arbitrary", "arbitrary"), vmem_limit_bytes=VMEM_LIMIT),
        name="odd_mixer",
    )(*args)


def _tail8(state):
    return jnp.pad(state, ((0, 0), (SUBLANES - state.shape[1], 0), (0, 0)))


def _pad_lanes(a):
    return jnp.pad(a, [(0, 0)] * (a.ndim - 1) + [(0, LANES - a.shape[-1])])


def _prepare(p):
    n_even = p["w_in_even"].shape[0]
    n_odd = p["w_in_odd"].shape[0]
    main = 3 * SC_WIDTH + 3 * FOX_WIDTH
    d_inner = p["w_out_odd"].shape[1]
    conv_dim = p["ssm_conv_w"].shape[2]
    row = lambda a: a.reshape(1, -1)
    even = [dict(
        win=p["w_in_even"][j, :, :main].astype(BF16),
        wf=_pad_lanes(p["w_in_even"][j, :, main:]).astype(BF16),
        cw=p["conv_a_w"][j],
        qg=row(jnp.tile(p["q_norm"][j], FOX_HEADS)),
        kg=row(jnp.tile(p["k_norm"][j], FOX_HEADS)),
        bf=_pad_lanes(row(p["b_forget"][j])),
        wout=p["w_out_even"][j].astype(BF16),
    ) for j in range(n_even)]
    odd = [dict(
        wz=p["w_in_odd"][j, :, :d_inner].astype(BF16),
        wx=p["w_in_odd"][j, :, d_inner:d_inner + conv_dim].astype(BF16),
        wdt=_pad_lanes(p["w_in_odd"][j, :, d_inner + conv_dim:]).astype(BF16),
        cw=p["ssm_conv_w"][j],
        cb=row(p["ssm_conv_b"][j]),
        dtb=_pad_lanes(row(p["dt_bias"][j])),
        alog=_pad_lanes(row(p["a_log"][j])),
        dsk=row(jnp.repeat(p["d_skip"][j], SSM_HEAD_DIM)),
        nw=row(p["ssm_norm"][j]),
        wout=p["w_out_odd"][j].astype(BF16),
    ) for j in range(n_odd)]
    ffn = [dict(
        gain=row(p["norm_ffn"][i]),
        wup=p["w_up"][i].astype(BF16),
        cw=p["ffn_conv_w"][i],
        cb=row(p["ffn_conv_b"][i]),
        wdn=p["w_down"][i].astype(BF16),
    ) for i in range(p["w_up"].shape[0])]
    mix_gain = [row(p["norm_mix"][i]) for i in range(p["norm_mix"].shape[0])]
    return even, odd, ffn, mix_gain


def _block_rows(seq):
    return min(EVEN_ROWS, seq), min(ODD_ROWS, seq), min(FFN_ROWS, seq)


def _trunk(x, cache_k, cache_v, cache_logf, st_sconv, st_ssm_conv, st_ssm, st_ffn, prep):
    even, odd, ffn, mix_gain = prep
    lb_even, lb_odd, lb_ffn = _block_rows(x.shape[1])
    b, l, _ = x.shape
    nlf, nsc, nsmc, nsm, nff = [], [], [], [], []
    kv_stack = None
    for i in range(len(ffn)):
        j = i // 2
        if i % 2 == 0:
            w = even[j]
            cache = None
            if cache_k is not None:
                clft = jnp.swapaxes(cache_logf[j], 1, 2)
                cache = (cache_k[j].reshape(b, -1, FOX_WIDTH), cache_v[j].reshape(b, -1, FOX_WIDTH), clft)
            x, k, v, lf, sc = _even_call(x, cache, _tail8(st_sconv[j]), mix_gain[i], w["win"], w["wf"], w["cw"],
                                         w["qg"], w["kg"], w["bf"], w["wout"], lb_even, j, len(even), kv_stack)
            kv_stack = (k, v)
            nlf.append(lf)
            nsc.append(sc[:, SUBLANES - st_sconv.shape[2]:])
        else:
            w = odd[j]
            x, cs, ss = _odd_call(x, _tail8(st_ssm_conv[j]), None if st_ssm is None else st_ssm[j], mix_gain[i],
                                  w["wz"], w["wx"], w["wdt"], w["cw"], w["cb"], w["dtb"], w["alog"], w["dsk"],
                                  w["nw"], w["wout"], lb_odd)
            nsmc.append(cs[:, SUBLANES - st_ssm_conv.shape[2]:])
            nsm.append(ss)
        w = ffn[i]
        x, fs = _ffn_call(x, _tail8(st_ffn[i]), w["gain"], w["wup"], w["cw"], w["cb"], w["wdn"], lb_ffn)
        nff.append(fs[:, SUBLANES - st_ffn.shape[2]:])
    heads_shape = (len(even), b, l, FOX_HEADS, HEAD_DIM)
    return (x, kv_stack[0].reshape(heads_shape), kv_stack[1].reshape(heads_shape), jnp.stack(nlf), jnp.stack(nsc),
            jnp.stack(nsmc), jnp.stack(nsm), jnp.stack(nff))


def kernel(x_prompt, x_sample, cache_fox_k, cache_fox_v, cache_fox_logf, state_sconv, state_ssm_conv, state_ssm, state_ffn_conv, norm_mix, norm_ffn, w_in_even, conv_a_w, q_norm, k_norm, b_forget, w_out_even, w_in_odd, ssm_conv_w, ssm_conv_b, dt_bias, a_log, d_skip, ssm_norm, w_out_odd, w_up, ffn_conv_w, ffn_conv_b, w_down):
    prep = _prepare(dict(
        norm_mix=norm_mix, norm_ffn=norm_ffn, w_in_even=w_in_even, conv_a_w=conv_a_w, q_norm=q_norm,
        k_norm=k_norm, b_forget=b_forget, w_out_even=w_out_even, w_in_odd=w_in_odd, ssm_conv_w=ssm_conv_w,
        ssm_conv_b=ssm_conv_b, dt_bias=dt_bias, a_log=a_log, d_skip=d_skip, ssm_norm=ssm_norm,
        w_out_odd=w_out_odd, w_up=w_up, ffn_conv_w=ffn_conv_w, ffn_conv_b=ffn_conv_b, w_down=w_down))
    bp = x_prompt.shape[0]
    zeros = lambda s: jnp.zeros((s.shape[0], bp) + s.shape[2:], s.dtype)
    p_out = _trunk(x_prompt, None, None, None, zeros(state_sconv), zeros(state_ssm_conv), None,
                   zeros(state_ffn_conv), prep)
    s_out = _trunk(x_sample, cache_fox_k, cache_fox_v, cache_fox_logf, state_sconv, state_ssm_conv, state_ssm,
                   state_ffn_conv, prep)
    return (p_out[0], s_out[0]) + p_out[1:] + s_out[1:]
```

```python
import functools
import math

import jax
import jax.numpy as jnp
from jax import lax
from jax.experimental import pallas as pl
from jax.experimental.pallas import tpu as pltpu

F32 = jnp.float32
BF16 = jnp.bfloat16
EPS = 1e-6
LOG2E = math.log2(math.e)

LANES = 128
SUBLANES = 8
MXU_DIM = 256

HEAD_DIM = 64
FOX_HEADS = 8
FOX_WIDTH = FOX_HEADS * HEAD_DIM
SC_WIDTH = 512
SSM_HEAD_DIM = 64
SSM_GROUPS = 4
D_STATE = 128
SSD_CHUNK = 128
VMEM_LIMIT = 60 * 1024 * 1024
EVEN_ROWS = 512
ODD_ROWS = 512
FFN_ROWS = 1024


def _resident(shape):
    return pl.BlockSpec(shape, lambda i, j: (0, 0), pipeline_mode=pl.Buffered(1))


def _dot(a, b):
    return jnp.dot(a, b, preferred_element_type=F32)


def _rms(x, gain):
    ms = jnp.mean(x * x, axis=-1, keepdims=True)
    return x * lax.rsqrt(ms + EPS) * gain


def _silu(x):
    return x / (1.0 + jnp.exp2(x * (-LOG2E)))


def _softplus(x):
    return jnp.maximum(x, 0.0) + jnp.log1p(jnp.exp(-jnp.abs(x)))


def _pad_rows(a, rows):
    if a.shape[0] == rows:
        return a
    return jnp.concatenate([a, jnp.zeros((rows - a.shape[0], a.shape[1]), a.dtype)], axis=0)


def _lane_cumsum(x, segment=None):
    n = x.shape[1] if segment is None else segment
    lane = lax.broadcasted_iota(jnp.int32, x.shape, 1)
    if segment is not None:
        lane = lane & (segment - 1)
    s = 1
    while s < n:
        x = x + jnp.where(lane >= s, pltpu.roll(x, s, axis=1), 0.0)
        s *= 2
    return x


def _causal_conv(buf, w, bias):
    width = w.shape[0]
    t = buf.shape[0] - SUBLANES
    y = buf[SUBLANES:SUBLANES + t, :] * w[width - 1:width]
    for d in range(1, width):
        y = y + buf[SUBLANES - d:SUBLANES - d + t, :] * w[width - 1 - d:width - d]
    if bias is not None:
        y = y + bias
    return y


def _row_cumsum(x):
    t = x.shape[0]
    tri = jnp.where(lax.broadcasted_iota(jnp.int32, (t, t), 1) <= lax.broadcasted_iota(jnp.int32, (t, t), 0),
                    1.0, 0.0).astype(BF16)
    hi = x.astype(BF16)
    rest = x - hi.astype(F32)
    mid = rest.astype(BF16)
    lo = (rest - mid.astype(F32)).astype(BF16)
    parts = _dot(tri, jnp.concatenate([hi, mid, lo], axis=1))
    return parts[:, 0:LANES] + parts[:, LANES:2 * LANES] + parts[:, 2 * LANES:3 * LANES]


def _pair_lane_mask(shape, e):
    lane = lax.broadcasted_iota(jnp.int32, shape, 1)
    return (lane < HEAD_DIM) if e == 0 else (lane >= HEAD_DIM)


def _ffn_body(x_ref, st_ref, g_ref, wup_ref, cw_ref, cb_ref, wdn_ref, o_ref, nst_ref,
              h_ref, a_ref, gt_ref, y_ref, *, lb, dff, cwid):
    @pl.when(pl.program_id(1) == 0)
    def _():
        nst_ref[...] = st_ref[...]

    x = x_ref[0]
    h_ref[...] = _rms(x, g_ref[...]).astype(BF16)
    n_chunks = dff // cwid

    def up(c):
        lo = c * cwid
        a_ref[c % 2, 0:SUBLANES, :] = nst_ref[0, :, lo:lo + cwid]
        a_ref[c % 2, SUBLANES:SUBLANES + lb, :] = _dot(h_ref[...], wup_ref[:, lo:lo + cwid])
        gt_ref[c % 2] = _dot(h_ref[...], wup_ref[:, dff + lo:dff + lo + cwid])

    up(0)
    for c in range(n_chunks):
        lo = c * cwid
        if c + 1 < n_chunks:
            up(c + 1)
        y = _causal_conv(a_ref.at[c % 2], cw_ref[:, lo:lo + cwid], cb_ref[:, lo:lo + cwid])
        nst_ref[0, :, lo:lo + cwid] = a_ref[c % 2, lb:lb + SUBLANES, :]
        y_ref[:, lo:lo + cwid] = (_silu(y) * gt_ref[c % 2]).astype(BF16)
    o_ref[0] = x + _dot(y_ref[...], wdn_ref[...])


def _ffn_call(x, st8, gain, wup, cw, cb, wdn, lb):
    b, l, d = x.shape
    dff = wdn.shape[0]
    const = lambda i, j: (0, 0)
    return pl.pallas_call(
        functools.partial(_ffn_body, lb=lb, dff=dff, cwid=MXU_DIM),
        grid=(b, l // lb),
        in_specs=[
            pl.BlockSpec((1, lb, d), lambda i, j: (i, j, 0)),
            pl.BlockSpec((1, SUBLANES, dff), lambda i, j: (i, 0, 0)),
            pl.BlockSpec((1, d), const),
            _resident((d, 2 * dff)),
            pl.BlockSpec(cw.shape, const),
            pl.BlockSpec((1, dff), const),
            _resident((dff, d)),
        ],
        out_specs=[
            pl.BlockSpec((1, lb, d), lambda i, j: (i, j, 0)),
            pl.BlockSpec((1, SUBLANES, dff), lambda i, j: (i, 0, 0)),
        ],
        out_shape=[jax.ShapeDtypeStruct(x.shape, F32), jax.ShapeDtypeStruct((b, SUBLANES, dff), F32)],
        scratch_shapes=[pltpu.VMEM((lb, d), BF16),
                        pltpu.VMEM((2, SUBLANES + lb, MXU_DIM), F32),
                        pltpu.VMEM((2, lb, MXU_DIM), F32),
                        pltpu.VMEM((lb, dff), BF16)],
        compiler_params=pltpu.CompilerParams(
            dimension_semantics=("arbitrary", "arbitrary"), vmem_limit_bytes=VMEM_LIMIT),
        name="conv_ffn",
    )(x, st8, gain, wup, cw, cb, wdn)


def _head_norm(q, gain):
    r = lax.shift_right_logical(lax.broadcasted_iota(jnp.int32, (MXU_DIM, MXU_DIM), 0), 6)
    c = lax.shift_right_logical(lax.broadcasted_iota(jnp.int32, (MXU_DIM, MXU_DIM), 1), 6)
    gmat = jnp.where(r == c, 1.0 / HEAD_DIM, 0.0).astype(BF16)
    qq = (q * q).astype(BF16)
    ms = jnp.concatenate([_dot(qq[:, i * MXU_DIM:(i + 1) * MXU_DIM], gmat)
                          for i in range(q.shape[1] // MXU_DIM)], axis=1)
    return q * lax.rsqrt(ms + EPS) * gain


def _even_body(*refs, lb, seq, past, tk, wd, n_aliased):
    refs = list(refs)
    x_ref = refs.pop(0)
    if past:
        ck_ref, cv_ref, clf_ref = refs[:3]
        del refs[:3]
    st_ref, g_ref, win_ref, wf_ref, cw_ref, qg_ref, kg_ref, bf_ref, wout_ref = refs[:9]
    del refs[:9 + n_aliased]
    (xo_ref, ko_ref, vo_ref, lfo_ref, nst_ref,
     k_s, vt_s, cbc_s, ccar_s, qm_s, cq_s, m_s, den_s, acc_s, t_s, h_s, proj_s, cu_s, cat_s) = refs
    blk = pl.program_id(1)

    @pl.when(blk == 0)
    def _():
        nst_ref[...] = st_ref[...]
        ccar_s[...] = jnp.zeros(ccar_s.shape, F32)
        if past:
            k_s[0:past, :] = ck_ref[0].astype(BF16)
            vt_s[:, 0:past] = cv_ref[0].T.astype(BF16)
            cpast = _lane_cumsum(clf_ref[0])
            cum_past = _pad_rows(cpast, LANES).T
            ccar_s[...] = jnp.broadcast_to(cum_past[past - 1:past, :], (SUBLANES, LANES))
            cum_past = cum_past * LOG2E
            for hh in range(FOX_HEADS):
                cbc_s[hh, 0:past, :] = jnp.broadcast_to(cum_past[:, hh:hh + 1], (past, LANES))

    x = x_ref[0]
    h_s[...] = _rms(x, g_ref[...]).astype(BF16)
    f_logit = _dot(h_s[...], wf_ref[...])
    for i in (3, 4, 5, 1, 2, 0):
        cols = slice(i * SC_WIDTH, (i + 1) * SC_WIDTH)
        proj_s[:, cols] = _dot(h_s[...], win_ref[:, cols])
    part = lambda i: proj_s[:, i * SC_WIDTH:(i + 1) * SC_WIDTH]

    qn = _head_norm(part(3), qg_ref[...])
    kn = _head_norm(part(4), kg_ref[...])
    v = part(5)
    ko_ref[0, 0] = kn
    vo_ref[0, 0] = v
    lane = lax.broadcasted_iota(jnp.int32, (lb, LANES), 1)
    logf = jnp.where(lane < FOX_HEADS, -_softplus(-(f_logit + bf_ref[...])), 0.0)
    lfo_ref[0] = logf[:, 0:FOX_HEADS]

    pos0 = past if seq == lb else pl.multiple_of(past + blk * lb, LANES)
    cum = _row_cumsum(_pad_rows(logf, wd)) + ccar_s[0:1, :]
    ccar_s[...] = jnp.broadcast_to(cum[lb - 1:lb, :], (SUBLANES, LANES))
    cum = cum * LOG2E
    cumt_blk = cum.T
    for hh in range(FOX_HEADS):
        cbc_s[hh, pl.ds(pos0, wd), :] = jnp.broadcast_to(cum[:, hh:hh + 1], (wd, LANES))
    k_s[pl.ds(pos0, wd), :] = _pad_rows(kn, wd).astype(BF16)
    vt_s[:, pl.ds(pos0, wd)] = _pad_rows(v, wd).T.astype(BF16)

    qt = _pad_rows(qn * (HEAD_DIM ** -0.5 * LOG2E), wd).T
    upper = lax.broadcasted_iota(jnp.int32, (LANES, wd), 0) < HEAD_DIM
    for g in range(FOX_HEADS // 2):
        qt_pair = qt[LANES * g:LANES * (g + 1)]
        qm_s[2 * g] = jnp.where(upper, qt_pair, 0.0).astype(BF16)
        qm_s[2 * g + 1] = jnp.where(upper, 0.0, qt_pair).astype(BF16)
    for hh in range(FOX_HEADS):
        cq_s[hh] = cumt_blk[hh:hh + 1]

    cu_s[0:SUBLANES, :] = nst_ref[0]
    cu_s[SUBLANES:SUBLANES + lb, :] = part(1) * part(2)
    cat_s[:, 0:SC_WIDTH] = (part(0) * _causal_conv(cu_s, cw_ref[...], None)).astype(BF16)
    nst_ref[0] = cu_s[lb:lb + SUBLANES, :]

    def stage_scores(hh, off, width, qlo=0):
        pair = slice(LANES * (hh // 2), LANES * (hh // 2 + 1))
        kq = _dot(k_s[pl.ds(off, width), pair], qm_s[hh, :, qlo:wd])
        ck = cbc_s[hh, pl.ds(off, width), :]
        t_s[hh, 0:width, qlo:wd] = kq - jnp.concatenate([ck] * ((wd - qlo) // LANES), axis=1)

    def softmax_update(hh, off, width, qlo=0, key0=None, first=False):
        lanes = slice(qlo, wd)
        cq = cq_s[hh, :, lanes]
        t = t_s[hh, 0:width, lanes]
        vt = vt_s[HEAD_DIM * hh:HEAD_DIM * (hh + 1), pl.ds(off, width)]
        if key0 is not None:
            shape = (width, wd - qlo)
            causal = (key0 + lax.broadcasted_iota(jnp.int32, shape, 0)
                      <= qlo + lax.broadcasted_iota(jnp.int32, shape, 1))
            t = jnp.where(causal, t, -jnp.inf)
        m = jnp.max(t, axis=0, keepdims=True) + cq
        if first:
            p = jnp.exp2(t + (cq - m))
            den_s[hh, :, lanes] = jnp.sum(p, axis=0, keepdims=True)
            acc_s[hh, :, lanes] = _dot(vt, p.astype(BF16))
        else:
            m_old = m_s[hh, :, lanes]
            m = jnp.maximum(m_old, m)
            alpha = jnp.exp2(m_old - m)
            p = jnp.exp2(t + (cq - m))
            den_s[hh, :, lanes] = alpha * den_s[hh, :, lanes] + jnp.sum(p, axis=0, keepdims=True)
            acc_s[hh, :, lanes] = alpha * acc_s[hh, :, lanes] + _dot(vt, p.astype(BF16))
        m_s[hh, :, lanes] = m

    n_prev = past // tk if seq == lb else past // tk + blk * (lb // tk)
    if wd >= 2 * LANES:
        half = wd // 2
        pos1 = pos0 + half if seq == lb else pl.multiple_of(pos0 + half, LANES)
        for hh in range(FOX_HEADS):
            stage_scores(hh, pos0, half)
        for hh in range(FOX_HEADS):
            softmax_update(hh, pos0, half, key0=0, first=True)
            stage_scores(hh, pos1, half, qlo=half)
        for hh in range(FOX_HEADS):
            softmax_update(hh, pos1, half, qlo=half, key0=half)
            stage_scores(hh, 0, tk)
    else:
        for hh in range(FOX_HEADS):
            stage_scores(hh, pos0, wd)
        for hh in range(FOX_HEADS):
            softmax_update(hh, pos0, wd, key0=0, first=True)
            stage_scores(hh, 0, tk)

    def prev_block(j, carry):
        off = pl.multiple_of(j * tk, tk)
        nxt = pl.multiple_of(jnp.minimum(j + 1, n_prev - 1) * tk, tk)
        for hh in range(FOX_HEADS):
            softmax_update(hh, off, tk)
            stage_scores(hh, nxt, tk)
        return carry

    lax.fori_loop(0, n_prev, prev_block, 0)
    attn_t = jnp.concatenate([acc_s[hh] / den_s[hh] for hh in range(FOX_HEADS)], axis=0)
    attn = attn_t.T[0:lb]

    cat_s[:, SC_WIDTH:SC_WIDTH + FOX_WIDTH] = attn.astype(BF16)
    xo_ref[0] = x + _dot(cat_s[...], wout_ref[...])


def _even_call(x, cache, st8, gain, win, wf, cw, qg, kg, bf, wout, lb, layer, n_layers, kv_stack):
    b, l, d = x.shape
    past = 0 if cache is None else cache[0].shape[1]
    wd = max(lb, LANES)
    tk = lb if past == 0 else min(past, 512)
    assert l % lb == 0 and past % tk == 0 and (l == lb or lb % tk == 0)
    total = past + max(l, wd)
    const = lambda i, j: (0, 0)
    tok = lambda width: pl.BlockSpec((1, lb, width), lambda i, j: (i, j, 0))
    in_specs = [tok(d)]
    args = [x]
    if past:
        ck, cv, clft = cache
        in_specs += [pl.BlockSpec((1, past, FOX_WIDTH), lambda i, j: (i, 0, 0)),
                     pl.BlockSpec((1, past, FOX_WIDTH), lambda i, j: (i, 0, 0)),
                     pl.BlockSpec((1, SUBLANES, past), lambda i, j: (i, 0, 0))]
        args += [ck, cv, clft]
    in_specs += [
        pl.BlockSpec((1, SUBLANES, SC_WIDTH), lambda i, j: (i, 0, 0)),
        pl.BlockSpec((1, d), const),
        _resident(win.shape),
        pl.BlockSpec(wf.shape, const),
        pl.BlockSpec(cw.shape, const),
        pl.BlockSpec((1, FOX_WIDTH), const),
        pl.BlockSpec((1, FOX_WIDTH), const),
        pl.BlockSpec((1, LANES), const),
        _resident(wout.shape),
    ]
    args += [st8, gain, win, wf, cw, qg, kg, bf, wout]
    aliases = {}
    if kv_stack is not None:
        aliases = {len(args): 1, len(args) + 1: 2}
        in_specs += [pl.BlockSpec(memory_space=pl.ANY)] * 2
        args += list(kv_stack)
    kv_spec = pl.BlockSpec((1, 1, lb, FOX_WIDTH), lambda i, j: (layer, i, j, 0))
    return pl.pallas_call(
        functools.partial(_even_body, lb=lb, seq=l, past=past, tk=tk, wd=wd, n_aliased=len(aliases)),
        grid=(b, l // lb),
        in_specs=in_specs,
        out_specs=[tok(d), kv_spec, kv_spec, tok(FOX_HEADS),
                   pl.BlockSpec((1, SUBLANES, SC_WIDTH), lambda i, j: (i, 0, 0))],
        out_shape=[jax.ShapeDtypeStruct(x.shape, F32),
                   jax.ShapeDtypeStruct((n_layers, b, l, FOX_WIDTH), F32),
                   jax.ShapeDtypeStruct((n_layers, b, l, FOX_WIDTH), F32),
                   jax.ShapeDtypeStruct((b, l, FOX_HEADS), F32),
                   jax.ShapeDtypeStruct((b, SUBLANES, SC_WIDTH), F32)],
        input_output_aliases=aliases,
        scratch_shapes=[pltpu.VMEM((total, FOX_WIDTH), BF16),
                        pltpu.VMEM((FOX_WIDTH, total), BF16),
                        pltpu.VMEM((FOX_HEADS, total, LANES), F32),
                        pltpu.VMEM((SUBLANES, LANES), F32),
                        pltpu.VMEM((FOX_HEADS, LANES, wd), BF16),
                        pltpu.VMEM((FOX_HEADS, 1, wd), F32),
                        pltpu.VMEM((FOX_HEADS, 1, wd), F32),
                        pltpu.VMEM((FOX_HEADS, 1, wd), F32),
                        pltpu.VMEM((FOX_HEADS, HEAD_DIM, wd), F32),
                        pltpu.VMEM((FOX_HEADS, max(tk, wd), wd), F32),
                        pltpu.VMEM((lb, d), BF16),
                        pltpu.VMEM((lb, win.shape[1]), F32),
                        pltpu.VMEM((SUBLANES + lb, SC_WIDTH), F32),
                        pltpu.VMEM((lb, SC_WIDTH + FOX_WIDTH), BF16)],
        compiler_params=pltpu.CompilerParams(
            dimension_semantics=("arbitrary", "arbitrary"), vmem_limit_bytes=VMEM_LIMIT),
        name="even_mixer",
    )(*args)


def _ssd_chunk(cumt, cum, dtt, w_end_t, xs_at, bm_at, cm_at, z_at, emit, s_ref, dsk_ref, nw_ref,
               between_groups=None):
    q = cum.shape[0]
    e_last = jnp.exp2(cum[q - 1:q, :])
    tri = (lax.broadcasted_iota(jnp.int32, (q, q), 1) <= lax.broadcasted_iota(jnp.int32, (q, q), 0))
    first_half = _pair_lane_mask((q, LANES), 0)
    first_half_n = _pair_lane_mask((D_STATE, LANES), 0)
    for g in range(SSM_GROUPS):
        n_cols = slice(D_STATE * g, D_STATE * (g + 1))
        cols = slice(4 * LANES * g, 4 * LANES * (g + 1))
        b_g = bm_at(n_cols)
        c_g = cm_at(n_cols)
        cb = lax.dot_general(c_g.astype(BF16), b_g.astype(BF16), (((1,), (1,)), ((), ())),
                             preferred_element_type=F32)
        b_t = b_g.T
        pieces = []
        for j in range(4):
            pair = 4 * g + j
            x_pair = xs_at(slice(LANES * pair, LANES * (pair + 1)))
            x_split = jnp.concatenate([jnp.where(first_half, x_pair, 0.0),
                                       jnp.where(first_half, 0.0, x_pair)], axis=0).astype(BF16)
            s_pair = s_ref[pair]
            s_split = jnp.concatenate([jnp.where(first_half_n, s_pair, 0.0),
                                       jnp.where(first_half_n, 0.0, s_pair)], axis=0).astype(BF16)
            mixes, c_scaled, b_scaled = [], [], []
            for e in range(2):
                hh = 2 * pair + e
                cum_col = jnp.broadcast_to(cum[:, hh:hh + 1], (q, LANES))
                decay = jnp.exp2(jnp.where(tri, cum_col - cumt[hh:hh + 1, :], -jnp.inf))
                mixes.append((cb * decay * dtt[hh:hh + 1, :]).astype(BF16))
                c_scaled.append((c_g * jnp.exp2(cum_col)).astype(BF16))
                b_scaled.append((b_t * w_end_t[hh:hh + 1, :]).astype(BF16))
            pieces.append(_dot(jnp.concatenate(mixes + c_scaled, axis=1),
                               jnp.concatenate([x_split, s_split], axis=0)))
            e_pair = jnp.where(_pair_lane_mask((1, LANES), 0),
                               e_last[:, 2 * pair:2 * pair + 1], e_last[:, 2 * pair + 1:2 * pair + 2])
            s_ref[pair] = s_pair * e_pair + _dot(jnp.concatenate(b_scaled, axis=1), x_split)
        y = jnp.concatenate(pieces, axis=1) + dsk_ref[:, cols] * xs_at(cols)
        y = y * _silu(z_at(cols))
        emit(cols, (_rms(y, nw_ref[:, cols])).astype(BF16))
        if between_groups is not None:
            between_groups(g)


def _odd_body(*refs, lb, has_state):
    if has_state:
        (x_ref, cst_ref, sst_ref, g_ref, wz_ref, wx_ref, wdt_ref, cw_ref, cb_ref, dtb_ref, alog_ref,
         dsk_ref, nw_ref, wout_ref, xo_ref, ncs_ref, nss_ref,
         s_s, yn_s, z_s, xbc_s, h_s, raw_s, cumt_s, cum_s, dtt_s, wendt_s) = refs
    else:
        (x_ref, cst_ref, g_ref, wz_ref, wx_ref, wdt_ref, cw_ref, cb_ref, dtb_ref, alog_ref,
         dsk_ref, nw_ref, wout_ref, xo_ref, ncs_ref, nss_ref,
         s_s, yn_s, z_s, xbc_s, h_s, raw_s, cumt_s, cum_s, dtt_s, wendt_s) = refs
    blk = pl.program_id(1)
    n_pairs = s_s.shape[0]

    @pl.when(blk == 0)
    def _():
        ncs_ref[...] = cst_ref[...]
        for pair in range(n_pairs):
            if has_state:
                s_s[pair] = sst_ref[0, 2 * pair:2 * pair + 2].reshape(2 * SSM_HEAD_DIM, D_STATE).T
            else:
                s_s[pair] = jnp.zeros((D_STATE, 2 * SSM_HEAD_DIM), F32)

    x = x_ref[0]
    h_s[...] = _rms(x, g_ref[...]).astype(BF16)
    cwid = raw_s.shape[2]
    n_chunks = wx_ref.shape[1] // cwid

    def project(c):
        cols = slice(cwid * c, cwid * (c + 1))
        raw_s[c % 2, 0:SUBLANES, :] = ncs_ref[0, :, cols]
        raw_s[c % 2, SUBLANES:SUBLANES + lb, :] = _dot(h_s[...], wx_ref[:, cols])

    d_inner = z_s.shape[1]
    z_chunks = d_inner // cwid
    project(0)
    dt = _softplus(_dot(h_s[...], wdt_ref[...]) + dtb_ref[...])
    for c in range(n_chunks):
        cols = slice(cwid * c, cwid * (c + 1))
        if c + 1 < n_chunks:
            project(c + 1)
        if c < z_chunks:
            z_s[:, cols] = _dot(h_s[...], wz_ref[:, cols])
        y = _causal_conv(raw_s.at[c % 2], cw_ref[:, cols], cb_ref[:, cols])
        ncs_ref[0, :, cols] = raw_s[c % 2, lb:lb + SUBLANES, :]
        xbc_s[:, cols] = _silu(y)

    q = SSD_CHUNK
    rows = max(lb, q)
    heads = d_inner // SSM_HEAD_DIM
    dt_t = _pad_rows(dt, rows).T[0:heads]
    da = dt * (-jnp.exp(alog_ref[...]) * LOG2E)
    cum_t = _lane_cumsum(_pad_rows(da, rows).T[0:heads], q)
    c_last = jnp.concatenate([jnp.broadcast_to(cum_t[:, q * (i + 1) - 1:q * (i + 1)], (heads, q))
                              for i in range(rows // q)], axis=1)
    dtt_s[...] = dt_t
    cumt_s[...] = cum_t
    wendt_s[...] = jnp.exp2(c_last - cum_t) * dt_t
    cum_s[...] = _pad_rows(cum_t, LANES).T

    bc_w = SSM_GROUPS * D_STATE
    if lb >= q:
        d_model = x_ref.shape[2]

        def project_out(c, part):
            r = pl.ds(pl.multiple_of(c * q, q), q)
            cols = slice(part * d_model // SSM_GROUPS, (part + 1) * d_model // SSM_GROUPS)
            xo_ref[0, r, cols] = x_ref[0, r, cols] + _dot(yn_s[r, :], wout_ref[:, cols])

        def ssd(c, between_groups=None):
            r = pl.ds(pl.multiple_of(c * q, q), q)

            outs = []
            _ssd_chunk(cumt_s[:, r], cum_s[r, :], dtt_s[:, r], wendt_s[:, r],
                       lambda cols: xbc_s[r, cols],
                       lambda cols: xbc_s[r, slice(d_inner + cols.start, d_inner + cols.stop)],
                       lambda cols: xbc_s[r, slice(d_inner + bc_w + cols.start, d_inner + bc_w + cols.stop)],
                       lambda cols: z_s[r, cols], lambda cols, value: outs.append(value),
                       s_s, dsk_ref, nw_ref, between_groups)
            yn_s[r, :] = jnp.concatenate(outs, axis=1)

        def chunk(c, carry):
            ssd(c, lambda g: project_out(c - 1, g))
            return carry

        ssd(0)
        lax.fori_loop(1, lb // q, chunk, 0)
        for part in range(SSM_GROUPS):
            project_out(lb // q - 1, part)
    else:
        def emit(cols, value):
            yn_s[:, cols] = value[0:lb]

        _ssd_chunk(cumt_s[...], cum_s[...], dtt_s[...], wendt_s[...],
                   lambda cols: _pad_rows(xbc_s[:, cols], q),
                   lambda cols: _pad_rows(xbc_s[:, slice(d_inner + cols.start, d_inner + cols.stop)], q),
                   lambda cols: _pad_rows(xbc_s[:, slice(d_inner + bc_w + cols.start, d_inner + bc_w + cols.stop)], q),
                   lambda cols: _pad_rows(z_s[:, cols], q), emit, s_s, dsk_ref, nw_ref)
        xo_ref[0] = x + _dot(yn_s[...], wout_ref[...])

    @pl.when(blk == pl.num_programs(1) - 1)
    def _():
        for pair in range(n_pairs):
            nss_ref[0, 2 * pair:2 * pair + 2] = s_s[pair].T.reshape(2, SSM_HEAD_DIM, D_STATE)


def _odd_call(x, cst8, sst, gain, wz, wx, wdt, cw, cb, dtb, alog, dsk, nw, wout, lb):
    b, l, d = x.shape
    d_inner = wz.shape[1]
    conv_dim = wx.shape[1]
    heads = d_inner // SSM_HEAD_DIM
    ssd_rows = max(lb, SSD_CHUNK)
    has_state = sst is not None
    const = lambda i, j: (0, 0)
    in_specs = [pl.BlockSpec((1, lb, d), lambda i, j: (i, j, 0)),
                pl.BlockSpec((1, SUBLANES, conv_dim), lambda i, j: (i, 0, 0))]
    args = [x, cst8]
    if has_state:
        in_specs.append(pl.BlockSpec((1, heads, SSM_HEAD_DIM, D_STATE), lambda i, j: (i, 0, 0, 0)))
        args.append(sst)
    in_specs += [
        pl.BlockSpec((1, d), const),
        _resident(wz.shape),
        _resident(wx.shape),
        pl.BlockSpec(wdt.shape, const),
        pl.BlockSpec(cw.shape, const),
        pl.BlockSpec((1, conv_dim), const),
        pl.BlockSpec((1, LANES), const),
        pl.BlockSpec((1, LANES), const),
        pl.BlockSpec((1, d_inner), const),
        pl.BlockSpec((1, d_inner), const),
        _resident(wout.shape),
    ]
    args += [gain, wz, wx, wdt, cw, cb, dtb, alog, dsk, nw, wout]
    return pl.pallas_call(
        functools.partial(_odd_body, lb=lb, has_state=has_state),
        grid=(b, l // lb),
        in_specs=in_specs,
        out_specs=[pl.BlockSpec((1, lb, d), lambda i, j: (i, j, 0)),
                   pl.BlockSpec((1, SUBLANES, conv_dim), lambda i, j: (i, 0, 0)),
                   pl.BlockSpec((1, heads, SSM_HEAD_DIM, D_STATE), lambda i, j: (i, 0, 0, 0))],
        out_shape=[jax.ShapeDtypeStruct(x.shape, F32),
                   jax.ShapeDtypeStruct((b, SUBLANES, conv_dim), F32),
                   jax.ShapeDtypeStruct((b, heads, SSM_HEAD_DIM, D_STATE), F32)],
        scratch_shapes=[pltpu.VMEM((heads // 2, D_STATE, 2 * SSM_HEAD_DIM), F32),
                        pltpu.VMEM((lb, d_inner), BF16),
                        pltpu.VMEM((lb, d_inner), F32),
                        pltpu.VMEM((lb, conv_dim), F32),
                        pltpu.VMEM((lb, d), BF16),
                        pltpu.VMEM((2, SUBLANES + lb, 4 * LANES), F32),
                        pltpu.VMEM((heads, ssd_rows), F32),
                        pltpu.VMEM((ssd_rows, LANES), F32),
                        pltpu.VMEM((heads, ssd_rows), F32),
                        pltpu.VMEM((heads, ssd_rows), F32)],
        compiler_params=pltpu.CompilerParams(
            dimension_semantics=("arbitrary", "arbitrary"), vmem_limit_bytes=VMEM_LIMIT),
        name="odd_mixer",
    )(*args)


def _tail8(state):
    return jnp.pad(state, ((0, 0), (SUBLANES - state.shape[1], 0), (0, 0)))


def _pad_lanes(a):
    return jnp.pad(a, [(0, 0)] * (a.ndim - 1) + [(0, LANES - a.shape[-1])])


def _prepare(p):
    n_even = p["w_in_even"].shape[0]
    n_odd = p["w_in_odd"].shape[0]
    main = 3 * SC_WIDTH + 3 * FOX_WIDTH
    d_inner = p["w_out_odd"].shape[1]
    conv_dim = p["ssm_conv_w"].shape[2]
    row = lambda a: a.reshape(1, -1)
    even = [dict(
        win=p["w_in_even"][j, :, :main].astype(BF16),
        wf=_pad_lanes(p["w_in_even"][j, :, main:]).astype(BF16),
        cw=p["conv_a_w"][j],
        qg=row(jnp.tile(p["q_norm"][j], FOX_HEADS)),
        kg=row(jnp.tile(p["k_norm"][j], FOX_HEADS)),
        bf=_pad_lanes(row(p["b_forget"][j])),
        wout=p["w_out_even"][j].astype(BF16),
    ) for j in range(n_even)]
    odd = [dict(
        wz=p["w_in_odd"][j, :, :d_inner].astype(BF16),
        wx=p["w_in_odd"][j, :, d_inner:d_inner + conv_dim].astype(BF16),
        wdt=_pad_lanes(p["w_in_odd"][j, :, d_inner + conv_dim:]).astype(BF16),
        cw=p["ssm_conv_w"][j],
        cb=row(p["ssm_conv_b"][j]),
        dtb=_pad_lanes(row(p["dt_bias"][j])),
        alog=_pad_lanes(row(p["a_log"][j])),
        dsk=row(jnp.repeat(p["d_skip"][j], SSM_HEAD_DIM)),
        nw=row(p["ssm_norm"][j]),
        wout=p["w_out_odd"][j].astype(BF16),
    ) for j in range(n_odd)]
    ffn = [dict(
        gain=row(p["norm_ffn"][i]),
        wup=p["w_up"][i].astype(BF16),
        cw=p["ffn_conv_w"][i],
        cb=row(p["ffn_conv_b"][i]),
        wdn=p["w_down"][i].astype(BF16),
    ) for i in range(p["w_up"].shape[0])]
    mix_gain = [row(p["norm_mix"][i]) for i in range(p["norm_mix"].shape[0])]
    return even, odd, ffn, mix_gain


def _block_rows(seq):
    return min(EVEN_ROWS, seq), min(ODD_ROWS, seq), min(FFN_ROWS, seq)


def _trunk(x, cache_k, cache_v, cache_logf, st_sconv, st_ssm_conv, st_ssm, st_ffn, prep):
    even, odd, ffn, mix_gain = prep
    lb_even, lb_odd, lb_ffn = _block_rows(x.shape[1])
    b, l, _ = x.shape
    nlf, nsc, nsmc, nsm, nff = [], [], [], [], []
    kv_stack = None
    for i in range(len(ffn)):
        j = i // 2
        if i % 2 == 0:
            w = even[j]
            cache = None
            if cache_k is not None:
                clft = jnp.swapaxes(cache_logf[j], 1, 2)
                cache = (cache_k[j].reshape(b, -1, FOX_WIDTH), cache_v[j].reshape(b, -1, FOX_WIDTH), clft)
            x, k, v, lf, sc = _even_call(x, cache, _tail8(st_sconv[j]), mix_gain[i], w["win"], w["wf"], w["cw"],
                                         w["qg"], w["kg"], w["bf"], w["wout"], lb_even, j, len(even), kv_stack)
            kv_stack = (k, v)
            nlf.append(lf)
            nsc.append(sc[:, SUBLANES - st_sconv.shape[2]:])
        else:
            w = odd[j]
            x, cs, ss = _odd_call(x, _tail8(st_ssm_conv[j]), None if st_ssm is None else st_ssm[j], mix_gain[i],
                                  w["wz"], w["wx"], w["wdt"], w["cw"], w["cb"], w["dtb"], w["alog"], w["dsk"],
                                  w["nw"], w["wout"], lb_odd)
            nsmc.append(cs[:, SUBLANES - st_ssm_conv.shape[2]:])
            nsm.append(ss)
        w = ffn[i]
        x, fs = _ffn_call(x, _tail8(st_ffn[i]), w["gain"], w["wup"], w["cw"], w["cb"], w["wdn"], lb_ffn)
        nff.append(fs[:, SUBLANES - st_ffn.shape[2]:])
    heads_shape = (len(even), b, l, FOX_HEADS, HEAD_DIM)
    return (x, kv_stack[0].reshape(heads_shape), kv_stack[1].reshape(heads_shape), jnp.stack(nlf), jnp.stack(nsc),
            jnp.stack(nsmc), jnp.stack(nsm), jnp.stack(nff))


def kernel(x_prompt, x_sample, cache_fox_k, cache_fox_v, cache_fox_logf, state_sconv, state_ssm_conv, state_ssm, state_ffn_conv, norm_mix, norm_ffn, w_in_even, conv_a_w, q_norm, k_norm, b_forget, w_out_even, w_in_odd, ssm_conv_w, ssm_conv_b, dt_bias, a_log, d_skip, ssm_norm, w_out_odd, w_up, ffn_conv_w, ffn_conv_b, w_down):
    prep = _prepare(dict(
        norm_mix=norm_mix, norm_ffn=norm_ffn, w_in_even=w_in_even, conv_a_w=conv_a_w, q_norm=q_norm,
        k_norm=k_norm, b_forget=b_forget, w_out_even=w_out_even, w_in_odd=w_in_odd, ssm_conv_w=ssm_conv_w,
        ssm_conv_b=ssm_conv_b, dt_bias=dt_bias, a_log=a_log, d_skip=d_skip, ssm_norm=ssm_norm,
        w_out_odd=w_out_odd, w_up=w_up, ffn_conv_w=ffn_conv_w, ffn_conv_b=ffn_conv_b, w_down=w_down))
    bp = x_prompt.shape[0]
    zeros = lambda s: jnp.zeros((s.shape[0], bp) + s.shape[2:], s.dtype)
    p_out = _trunk(x_prompt, None, None, None, zeros(state_sconv), zeros(state_ssm_conv), None,
                   zeros(state_ffn_conv), prep)
    s_out = _trunk(x_sample, cache_fox_k, cache_fox_v, cache_fox_logf, state_sconv, state_ssm_conv, state_ssm,
                   state_ffn_conv, prep)
    return (p_out[0], s_out[0]) + p_out[1:] + s_out[1:]
```

```python
import functools
import math

import jax
import jax.numpy as jnp
from jax import lax
from jax.experimental import pallas as pl
from jax.experimental.pallas import tpu as pltpu

F32 = jnp.float32
BF16 = jnp.bfloat16
EPS = 1e-6
LOG2E = math.log2(math.e)

LANES = 128
SUBLANES = 8
MXU_DIM = 256

HEAD_DIM = 64
FOX_HEADS = 8
FOX_WIDTH = FOX_HEADS * HEAD_DIM
SC_WIDTH = 512
SSM_HEAD_DIM = 64
SSM_GROUPS = 4
D_STATE = 128
SSD_CHUNK = 128
VMEM_LIMIT = 60 * 1024 * 1024
EVEN_ROWS = 512
ODD_ROWS = 512
FFN_ROWS = 1024


def _resident(shape):
    return pl.BlockSpec(shape, lambda i, j: (0, 0), pipeline_mode=pl.Buffered(1))


def _dot(a, b):
    return jnp.dot(a, b, preferred_element_type=F32)


def _rms(x, gain):
    ms = jnp.mean(x * x, axis=-1, keepdims=True)
    return x * lax.rsqrt(ms + EPS) * gain


def _silu(x):
    return x / (1.0 + jnp.exp2(x * (-LOG2E)))


def _softplus(x):
    return jnp.maximum(x, 0.0) + jnp.log1p(jnp.exp(-jnp.abs(x)))


def _pad_rows(a, rows):
    if a.shape[0] == rows:
        return a
    return jnp.concatenate([a, jnp.zeros((rows - a.shape[0], a.shape[1]), a.dtype)], axis=0)


def _lane_cumsum(x, segment=None):
    n = x.shape[1] if segment is None else segment
    lane = lax.broadcasted_iota(jnp.int32, x.shape, 1)
    if segment is not None:
        lane = lane & (segment - 1)
    s = 1
    while s < n:
        x = x + jnp.where(lane >= s, pltpu.roll(x, s, axis=1), 0.0)
        s *= 2
    return x


def _causal_conv(buf, w, bias):
    width = w.shape[0]
    t = buf.shape[0] - SUBLANES
    y = buf[SUBLANES:SUBLANES + t, :] * w[width - 1:width]
    for d in range(1, width):
        y = y + buf[SUBLANES - d:SUBLANES - d + t, :] * w[width - 1 - d:width - d]
    if bias is not None:
        y = y + bias
    return y


def _row_cumsum(x):
    t = x.shape[0]
    tri = jnp.where(lax.broadcasted_iota(jnp.int32, (t, t), 1) <= lax.broadcasted_iota(jnp.int32, (t, t), 0),
                    1.0, 0.0).astype(BF16)
    hi = x.astype(BF16)
    rest = x - hi.astype(F32)
    mid = rest.astype(BF16)
    lo = (rest - mid.astype(F32)).astype(BF16)
    parts = _dot(tri, jnp.concatenate([hi, mid, lo], axis=1))
    return parts[:, 0:LANES] + parts[:, LANES:2 * LANES] + parts[:, 2 * LANES:3 * LANES]


def _pair_lane_mask(shape, e):
    lane = lax.broadcasted_iota(jnp.int32, shape, 1)
    return (lane < HEAD_DIM) if e == 0 else (lane >= HEAD_DIM)


def _ffn_body(x_ref, st_ref, g_ref, wup_ref, cw_ref, cb_ref, wdn_ref, o_ref, nst_ref,
              h_ref, a_ref, gt_ref, y_ref, *, lb, dff, cwid):
    @pl.when(pl.program_id(1) == 0)
    def _():
        nst_ref[...] = st_ref[...]

    x = x_ref[0]
    h_ref[...] = _rms(x, g_ref[...]).astype(BF16)
    n_chunks = dff // cwid

    def up(c):
        lo = c * cwid
        a_ref[c % 2, 0:SUBLANES, :] = nst_ref[0, :, lo:lo + cwid]
        a_ref[c % 2, SUBLANES:SUBLANES + lb, :] = _dot(h_ref[...], wup_ref[:, lo:lo + cwid])
        gt_ref[c % 2] = _dot(h_ref[...], wup_ref[:, dff + lo:dff + lo + cwid])

    up(0)
    for c in range(n_chunks):
        lo = c * cwid
        if c + 1 < n_chunks:
            up(c + 1)
        y = _causal_conv(a_ref.at[c % 2], cw_ref[:, lo:lo + cwid], cb_ref[:, lo:lo + cwid])
        nst_ref[0, :, lo:lo + cwid] = a_ref[c % 2, lb:lb + SUBLANES, :]
        y_ref[:, lo:lo + cwid] = (_silu(y) * gt_ref[c % 2]).astype(BF16)
    o_ref[0] = x + _dot(y_ref[...], wdn_ref[...])


def _ffn_call(x, st8, gain, wup, cw, cb, wdn, lb):
    b, l, d = x.shape
    dff = wdn.shape[0]
    const = lambda i, j: (0, 0)
    return pl.pallas_call(
        functools.partial(_ffn_body, lb=lb, dff=dff, cwid=MXU_DIM),
        grid=(b, l // lb),
        in_specs=[
            pl.BlockSpec((1, lb, d), lambda i, j: (i, j, 0)),
            pl.BlockSpec((1, SUBLANES, dff), lambda i, j: (i, 0, 0)),
            pl.BlockSpec((1, d), const),
            _resident((d, 2 * dff)),
            pl.BlockSpec(cw.shape, const),
            pl.BlockSpec((1, dff), const),
            _resident((dff, d)),
        ],
        out_specs=[
            pl.BlockSpec((1, lb, d), lambda i, j: (i, j, 0)),
            pl.BlockSpec((1, SUBLANES, dff), lambda i, j: (i, 0, 0)),
        ],
        out_shape=[jax.ShapeDtypeStruct(x.shape, F32), jax.ShapeDtypeStruct((b, SUBLANES, dff), F32)],
        scratch_shapes=[pltpu.VMEM((lb, d), BF16),
                        pltpu.VMEM((2, SUBLANES + lb, MXU_DIM), F32),
                        pltpu.VMEM((2, lb, MXU_DIM), F32),
                        pltpu.VMEM((lb, dff), BF16)],
        compiler_params=pltpu.CompilerParams(
            dimension_semantics=("arbitrary", "arbitrary"), vmem_limit_bytes=VMEM_LIMIT),
        name="conv_ffn",
    )(x, st8, gain, wup, cw, cb, wdn)


def _head_norm(q, gain):
    r = lax.shift_right_logical(lax.broadcasted_iota(jnp.int32, (MXU_DIM, MXU_DIM), 0), 6)
    c = lax.shift_right_logical(lax.broadcasted_iota(jnp.int32, (MXU_DIM, MXU_DIM), 1), 6)
    gmat = jnp.where(r == c, 1.0 / HEAD_DIM, 0.0).astype(BF16)
    qq = (q * q).astype(BF16)
    ms = jnp.concatenate([_dot(qq[:, i * MXU_DIM:(i + 1) * MXU_DIM], gmat)
                          for i in range(q.shape[1] // MXU_DIM)], axis=1)
    return q * lax.rsqrt(ms + EPS) * gain


def _even_body(*refs, lb, seq, past, tk, wd):
    refs = list(refs)
    x_ref = refs.pop(0)
    if past:
        ck_ref, cv_ref, clf_ref = refs[:3]
        del refs[:3]
    st_ref, g_ref, win_ref, wf_ref, cw_ref, qg_ref, kg_ref, bf_ref, wout_ref = refs[:9]
    del refs[:9]
    (xo_ref, ko_ref, vo_ref, lfo_ref, nst_ref,
     k_s, vt_s, cbc_s, ccar_s, qm_s, cq_s, m_s, den_s, acc_s, t_s, h_s, proj_s, cu_s, cat_s) = refs
    blk = pl.program_id(1)

    @pl.when(blk == 0)
    def _():
        nst_ref[...] = st_ref[...]
        ccar_s[...] = jnp.zeros(ccar_s.shape, F32)
        if past:
            k_s[0:past, :] = ck_ref[0].astype(BF16)
            vt_s[:, 0:past] = cv_ref[0].T.astype(BF16)
            cpast = _lane_cumsum(clf_ref[0])
            cum_past = _pad_rows(cpast, LANES).T
            ccar_s[...] = jnp.broadcast_to(cum_past[past - 1:past, :], (SUBLANES, LANES))
            cum_past = cum_past * LOG2E
            for hh in range(FOX_HEADS):
                cbc_s[hh, 0:past, :] = jnp.broadcast_to(cum_past[:, hh:hh + 1], (past, LANES))

    x = x_ref[0]
    h_s[...] = _rms(x, g_ref[...]).astype(BF16)
    f_logit = _dot(h_s[...], wf_ref[...])
    for i in (3, 4, 5, 1, 2, 0):
        cols = slice(i * SC_WIDTH, (i + 1) * SC_WIDTH)
        proj_s[:, cols] = _dot(h_s[...], win_ref[:, cols])
    part = lambda i: proj_s[:, i * SC_WIDTH:(i + 1) * SC_WIDTH]

    qn = _head_norm(part(3), qg_ref[...])
    kn = _head_norm(part(4), kg_ref[...])
    v = part(5)
    ko_ref[0] = kn
    vo_ref[0] = v
    lane = lax.broadcasted_iota(jnp.int32, (lb, LANES), 1)
    logf = jnp.where(lane < FOX_HEADS, -_softplus(-(f_logit + bf_ref[...])), 0.0)
    lfo_ref[0] = logf[:, 0:FOX_HEADS]

    pos0 = past if seq == lb else pl.multiple_of(past + blk * lb, LANES)
    cum = _row_cumsum(_pad_rows(logf, wd)) + ccar_s[0:1, :]
    ccar_s[...] = jnp.broadcast_to(cum[lb - 1:lb, :], (SUBLANES, LANES))
    cum = cum * LOG2E
    cumt_blk = cum.T
    for hh in range(FOX_HEADS):
        cbc_s[hh, pl.ds(pos0, wd), :] = jnp.broadcast_to(cum[:, hh:hh + 1], (wd, LANES))
    k_s[pl.ds(pos0, wd), :] = _pad_rows(kn, wd).astype(BF16)
    vt_s[:, pl.ds(pos0, wd)] = _pad_rows(v, wd).T.astype(BF16)

    qt = _pad_rows(qn * (HEAD_DIM ** -0.5 * LOG2E), wd).T
    upper = lax.broadcasted_iota(jnp.int32, (LANES, wd), 0) < HEAD_DIM
    for g in range(FOX_HEADS // 2):
        qt_pair = qt[LANES * g:LANES * (g + 1)]
        qm_s[2 * g] = jnp.where(upper, qt_pair, 0.0).astype(BF16)
        qm_s[2 * g + 1] = jnp.where(upper, 0.0, qt_pair).astype(BF16)
    for hh in range(FOX_HEADS):
        cq_s[hh] = cumt_blk[hh:hh + 1]

    cu_s[0:SUBLANES, :] = nst_ref[0]
    cu_s[SUBLANES:SUBLANES + lb, :] = part(1) * part(2)
    cat_s[:, 0:SC_WIDTH] = (part(0) * _causal_conv(cu_s, cw_ref[...], None)).astype(BF16)
    nst_ref[0] = cu_s[lb:lb + SUBLANES, :]

    def stage_scores(hh, off, width, qlo=0):
        pair = slice(LANES * (hh // 2), LANES * (hh // 2 + 1))
        kq = _dot(k_s[pl.ds(off, width), pair], qm_s[hh, :, qlo:wd])
        ck = cbc_s[hh, pl.ds(off, width), :]
        t_s[hh, 0:width, qlo:wd] = kq - jnp.concatenate([ck] * ((wd - qlo) // LANES), axis=1)

    def softmax_update(hh, off, width, qlo=0, key0=None, first=False):
        lanes = slice(qlo, wd)
        cq = cq_s[hh, :, lanes]
        t = t_s[hh, 0:width, lanes]
        vt = vt_s[HEAD_DIM * hh:HEAD_DIM * (hh + 1), pl.ds(off, width)]
        if key0 is not None:
            shape = (width, wd - qlo)
            causal = (key0 + lax.broadcasted_iota(jnp.int32, shape, 0)
                      <= qlo + lax.broadcasted_iota(jnp.int32, shape, 1))
            t = jnp.where(causal, t, -jnp.inf)
        m = jnp.max(t, axis=0, keepdims=True) + cq
        if first:
            p = jnp.exp2(t + (cq - m))
            den_s[hh, :, lanes] = jnp.sum(p, axis=0, keepdims=True)
            acc_s[hh, :, lanes] = _dot(vt, p.astype(BF16))
        else:
            m_old = m_s[hh, :, lanes]
            m = jnp.maximum(m_old, m)
            alpha = jnp.exp2(m_old - m)
            p = jnp.exp2(t + (cq - m))
            den_s[hh, :, lanes] = alpha * den_s[hh, :, lanes] + jnp.sum(p, axis=0, keepdims=True)
            acc_s[hh, :, lanes] = alpha * acc_s[hh, :, lanes] + _dot(vt, p.astype(BF16))
        m_s[hh, :, lanes] = m

    n_prev = past // tk if seq == lb else past // tk + blk * (lb // tk)
    if wd >= 2 * LANES:
        half = wd // 2
        pos1 = pos0 + half if seq == lb else pl.multiple_of(pos0 + half, LANES)
        for hh in range(FOX_HEADS):
            stage_scores(hh, pos0, half)
        for hh in range(FOX_HEADS):
            softmax_update(hh, pos0, half, key0=0, first=True)
            stage_scores(hh, pos1, half, qlo=half)
        for hh in range(FOX_HEADS):
            softmax_update(hh, pos1, half, qlo=half, key0=half)
            stage_scores(hh, 0, tk)
    else:
        for hh in range(FOX_HEADS):
            stage_scores(hh, pos0, wd)
        for hh in range(FOX_HEADS):
            softmax_update(hh, pos0, wd, key0=0, first=True)
            stage_scores(hh, 0, tk)

    def prev_block(j, carry):
        off = pl.multiple_of(j * tk, tk)
        nxt = pl.multiple_of(jnp.minimum(j + 1, n_prev - 1) * tk, tk)
        for hh in range(FOX_HEADS):
            softmax_update(hh, off, tk)
            stage_scores(hh, nxt, tk)
        return carry

    lax.fori_loop(0, n_prev, prev_block, 0)
    attn_t = jnp.concatenate([acc_s[hh] / den_s[hh] for hh in range(FOX_HEADS)], axis=0)
    attn = attn_t.T[0:lb]

    cat_s[:, SC_WIDTH:SC_WIDTH + FOX_WIDTH] = attn.astype(BF16)
    xo_ref[0] = x + _dot(cat_s[...], wout_ref[...])


def _even_call(x, cache, st8, gain, win, wf, cw, qg, kg, bf, wout, lb):
    b, l, d = x.shape
    past = 0 if cache is None else cache[0].shape[1]
    wd = max(lb, LANES)
    tk = lb if past == 0 else min(past, 512)
    assert l % lb == 0 and past % tk == 0 and (l == lb or lb % tk == 0)
    total = past + max(l, wd)
    const = lambda i, j: (0, 0)
    tok = lambda width: pl.BlockSpec((1, lb, width), lambda i, j: (i, j, 0))
    in_specs = [tok(d)]
    args = [x]
    if past:
        ck, cv, clft = cache
        in_specs += [pl.BlockSpec((1, past, FOX_WIDTH), lambda i, j: (i, 0, 0)),
                     pl.BlockSpec((1, past, FOX_WIDTH), lambda i, j: (i, 0, 0)),
                     pl.BlockSpec((1, SUBLANES, past), lambda i, j: (i, 0, 0))]
        args += [ck, cv, clft]
    in_specs += [
        pl.BlockSpec((1, SUBLANES, SC_WIDTH), lambda i, j: (i, 0, 0)),
        pl.BlockSpec((1, d), const),
        _resident(win.shape),
        pl.BlockSpec(wf.shape, const),
        pl.BlockSpec(cw.shape, const),
        pl.BlockSpec((1, FOX_WIDTH), const),
        pl.BlockSpec((1, FOX_WIDTH), const),
        pl.BlockSpec((1, LANES), const),
        _resident(wout.shape),
    ]
    args += [st8, gain, win, wf, cw, qg, kg, bf, wout]
    return pl.pallas_call(
        functools.partial(_even_body, lb=lb, seq=l, past=past, tk=tk, wd=wd),
        grid=(b, l // lb),
        in_specs=in_specs,
        out_specs=[tok(d), tok(FOX_WIDTH), tok(FOX_WIDTH), tok(FOX_HEADS),
                   pl.BlockSpec((1, SUBLANES, SC_WIDTH), lambda i, j: (i, 0, 0))],
        out_shape=[jax.ShapeDtypeStruct(x.shape, F32),
                   jax.ShapeDtypeStruct((b, l, FOX_WIDTH), F32),
                   jax.ShapeDtypeStruct((b, l, FOX_WIDTH), F32),
                   jax.ShapeDtypeStruct((b, l, FOX_HEADS), F32),
                   jax.ShapeDtypeStruct((b, SUBLANES, SC_WIDTH), F32)],
        scratch_shapes=[pltpu.VMEM((total, FOX_WIDTH), BF16),
                        pltpu.VMEM((FOX_WIDTH, total), BF16),
                        pltpu.VMEM((FOX_HEADS, total, LANES), F32),
                        pltpu.VMEM((SUBLANES, LANES), F32),
                        pltpu.VMEM((FOX_HEADS, LANES, wd), BF16),
                        pltpu.VMEM((FOX_HEADS, 1, wd), F32),
                        pltpu.VMEM((FOX_HEADS, 1, wd), F32),
                        pltpu.VMEM((FOX_HEADS, 1, wd), F32),
                        pltpu.VMEM((FOX_HEADS, HEAD_DIM, wd), F32),
                        pltpu.VMEM((FOX_HEADS, max(tk, wd), wd), F32),
                        pltpu.VMEM((lb, d), BF16),
                        pltpu.VMEM((lb, win.shape[1]), F32),
                        pltpu.VMEM((SUBLANES + lb, SC_WIDTH), F32),
                        pltpu.VMEM((lb, SC_WIDTH + FOX_WIDTH), BF16)],
        compiler_params=pltpu.CompilerParams(
            dimension_semantics=("arbitrary", "arbitrary"), vmem_limit_bytes=VMEM_LIMIT),
        name="even_mixer",
    )(*args)


def _ssd_chunk(cumt, cum, dtt, w_end_t, xs_at, bm_at, cm_at, z_at, emit, s_ref, dsk_ref, nw_ref,
               between_groups=None):
    q = cum.shape[0]
    e_last = jnp.exp2(cum[q - 1:q, :])
    tri = (lax.broadcasted_iota(jnp.int32, (q, q), 1) <= lax.broadcasted_iota(jnp.int32, (q, q), 0))
    first_half = _pair_lane_mask((q, LANES), 0)
    first_half_n = _pair_lane_mask((D_STATE, LANES), 0)
    for g in range(SSM_GROUPS):
        n_cols = slice(D_STATE * g, D_STATE * (g + 1))
        cols = slice(4 * LANES * g, 4 * LANES * (g + 1))
        b_g = bm_at(n_cols)
        c_g = cm_at(n_cols)
        cb = lax.dot_general(c_g.astype(BF16), b_g.astype(BF16), (((1,), (1,)), ((), ())),
                             preferred_element_type=F32)
        b_t = b_g.T
        pieces = []
        for j in range(4):
            pair = 4 * g + j
            x_pair = xs_at(slice(LANES * pair, LANES * (pair + 1)))
            x_split = jnp.concatenate([jnp.where(first_half, x_pair, 0.0),
                                       jnp.where(first_half, 0.0, x_pair)], axis=0).astype(BF16)
            s_pair = s_ref[pair]
            s_split = jnp.concatenate([jnp.where(first_half_n, s_pair, 0.0),
                                       jnp.where(first_half_n, 0.0, s_pair)], axis=0).astype(BF16)
            mixes, c_scaled, b_scaled = [], [], []
            for e in range(2):
                hh = 2 * pair + e
                cum_col = jnp.broadcast_to(cum[:, hh:hh + 1], (q, LANES))
                decay = jnp.exp2(jnp.where(tri, cum_col - cumt[hh:hh + 1, :], -jnp.inf))
                mixes.append((cb * decay * dtt[hh:hh + 1, :]).astype(BF16))
                c_scaled.append((c_g * jnp.exp2(cum_col)).astype(BF16))
                b_scaled.append((b_t * w_end_t[hh:hh + 1, :]).astype(BF16))
            pieces.append(_dot(jnp.concatenate(mixes + c_scaled, axis=1),
                               jnp.concatenate([x_split, s_split], axis=0)))
            e_pair = jnp.where(_pair_lane_mask((1, LANES), 0),
                               e_last[:, 2 * pair:2 * pair + 1], e_last[:, 2 * pair + 1:2 * pair + 2])
            s_ref[pair] = s_pair * e_pair + _dot(jnp.concatenate(b_scaled, axis=1), x_split)
        y = jnp.concatenate(pieces, axis=1) + dsk_ref[:, cols] * xs_at(cols)
        y = y * _silu(z_at(cols))
        emit(cols, (_rms(y, nw_ref[:, cols])).astype(BF16))
        if between_groups is not None:
            between_groups(g)


def _odd_body(*refs, lb, has_state):
    if has_state:
        (x_ref, cst_ref, sst_ref, g_ref, wz_ref, wx_ref, wdt_ref, cw_ref, cb_ref, dtb_ref, alog_ref,
         dsk_ref, nw_ref, wout_ref, xo_ref, ncs_ref, nss_ref,
         s_s, yn_s, z_s, xbc_s, h_s, raw_s, cumt_s, cum_s, dtt_s, wendt_s) = refs
    else:
        (x_ref, cst_ref, g_ref, wz_ref, wx_ref, wdt_ref, cw_ref, cb_ref, dtb_ref, alog_ref,
         dsk_ref, nw_ref, wout_ref, xo_ref, ncs_ref, nss_ref,
         s_s, yn_s, z_s, xbc_s, h_s, raw_s, cumt_s, cum_s, dtt_s, wendt_s) = refs
    blk = pl.program_id(1)
    n_pairs = s_s.shape[0]

    @pl.when(blk == 0)
    def _():
        ncs_ref[...] = cst_ref[...]
        for pair in range(n_pairs):
            if has_state:
                s_s[pair] = sst_ref[0, 2 * pair:2 * pair + 2].reshape(2 * SSM_HEAD_DIM, D_STATE).T
            else:
                s_s[pair] = jnp.zeros((D_STATE, 2 * SSM_HEAD_DIM), F32)

    x = x_ref[0]
    h_s[...] = _rms(x, g_ref[...]).astype(BF16)
    cwid = raw_s.shape[2]
    n_chunks = wx_ref.shape[1] // cwid

    def project(c):
        cols = slice(cwid * c, cwid * (c + 1))
        raw_s[c % 2, 0:SUBLANES, :] = ncs_ref[0, :, cols]
        raw_s[c % 2, SUBLANES:SUBLANES + lb, :] = _dot(h_s[...], wx_ref[:, cols])

    d_inner = z_s.shape[1]
    z_chunks = d_inner // cwid
    project(0)
    dt = _softplus(_dot(h_s[...], wdt_ref[...]) + dtb_ref[...])
    for c in range(n_chunks):
        cols = slice(cwid * c, cwid * (c + 1))
        if c + 1 < n_chunks:
            project(c + 1)
        if c < z_chunks:
            z_s[:, cols] = _dot(h_s[...], wz_ref[:, cols])
        y = _causal_conv(raw_s.at[c % 2], cw_ref[:, cols], cb_ref[:, cols])
        ncs_ref[0, :, cols] = raw_s[c % 2, lb:lb + SUBLANES, :]
        xbc_s[:, cols] = _silu(y)

    q = SSD_CHUNK
    rows = max(lb, q)
    heads = d_inner // SSM_HEAD_DIM
    dt_t = _pad_rows(dt, rows).T[0:heads]
    da = dt * (-jnp.exp(alog_ref[...]) * LOG2E)
    cum_t = _lane_cumsum(_pad_rows(da, rows).T[0:heads], q)
    c_last = jnp.concatenate([jnp.broadcast_to(cum_t[:, q * (i + 1) - 1:q * (i + 1)], (heads, q))
                              for i in range(rows // q)], axis=1)
    dtt_s[...] = dt_t
    cumt_s[...] = cum_t
    wendt_s[...] = jnp.exp2(c_last - cum_t) * dt_t
    cum_s[...] = _pad_rows(cum_t, LANES).T

    bc_w = SSM_GROUPS * D_STATE
    if lb >= q:
        d_model = x_ref.shape[2]

        def project_out(c, part):
            r = pl.ds(pl.multiple_of(c * q, q), q)
            cols = slice(part * d_model // SSM_GROUPS, (part + 1) * d_model // SSM_GROUPS)
            xo_ref[0, r, cols] = x_ref[0, r, cols] + _dot(yn_s[r, :], wout_ref[:, cols])

        def ssd(c, between_groups=None):
            r = pl.ds(pl.multiple_of(c * q, q), q)

            outs = []
            _ssd_chunk(cumt_s[:, r], cum_s[r, :], dtt_s[:, r], wendt_s[:, r],
                       lambda cols: xbc_s[r, cols],
                       lambda cols: xbc_s[r, slice(d_inner + cols.start, d_inner + cols.stop)],
                       lambda cols: xbc_s[r, slice(d_inner + bc_w + cols.start, d_inner + bc_w + cols.stop)],
                       lambda cols: z_s[r, cols], lambda cols, value: outs.append(value),
                       s_s, dsk_ref, nw_ref, between_groups)
            yn_s[r, :] = jnp.concatenate(outs, axis=1)

        def chunk(c, carry):
            ssd(c, lambda g: project_out(c - 1, g))
            return carry

        ssd(0)
        lax.fori_loop(1, lb // q, chunk, 0)
        for part in range(SSM_GROUPS):
            project_out(lb // q - 1, part)
    else:
        def emit(cols, value):
            yn_s[:, cols] = value[0:lb]

        _ssd_chunk(cumt_s[...], cum_s[...], dtt_s[...], wendt_s[...],
                   lambda cols: _pad_rows(xbc_s[:, cols], q),
                   lambda cols: _pad_rows(xbc_s[:, slice(d_inner + cols.start, d_inner + cols.stop)], q),
                   lambda cols: _pad_rows(xbc_s[:, slice(d_inner + bc_w + cols.start, d_inner + bc_w + cols.stop)], q),
                   lambda cols: _pad_rows(z_s[:, cols], q), emit, s_s, dsk_ref, nw_ref)
        xo_ref[0] = x + _dot(yn_s[...], wout_ref[...])

    @pl.when(blk == pl.num_programs(1) - 1)
    def _():
        for pair in range(n_pairs):
            nss_ref[0, 2 * pair:2 * pair + 2] = s_s[pair].T.reshape(2, SSM_HEAD_DIM, D_STATE)


def _odd_call(x, cst8, sst, gain, wz, wx, wdt, cw, cb, dtb, alog, dsk, nw, wout, lb):
    b, l, d = x.shape
    d_inner = wz.shape[1]
    conv_dim = wx.shape[1]
    heads = d_inner // SSM_HEAD_DIM
    ssd_rows = max(lb, SSD_CHUNK)
    has_state = sst is not None
    const = lambda i, j: (0, 0)
    in_specs = [pl.BlockSpec((1, lb, d), lambda i, j: (i, j, 0)),
                pl.BlockSpec((1, SUBLANES, conv_dim), lambda i, j: (i, 0, 0))]
    args = [x, cst8]
    if has_state:
        in_specs.append(pl.BlockSpec((1, heads, SSM_HEAD_DIM, D_STATE), lambda i, j: (i, 0, 0, 0)))
        args.append(sst)
    in_specs += [
        pl.BlockSpec((1, d), const),
        _resident(wz.shape),
        _resident(wx.shape),
        pl.BlockSpec(wdt.shape, const),
        pl.BlockSpec(cw.shape, const),
        pl.BlockSpec((1, conv_dim), const),
        pl.BlockSpec((1, LANES), const),
        pl.BlockSpec((1, LANES), const),
        pl.BlockSpec((1, d_inner), const),
        pl.BlockSpec((1, d_inner), const),
        _resident(wout.shape),
    ]
    args += [gain, wz, wx, wdt, cw, cb, dtb, alog, dsk, nw, wout]
    return pl.pallas_call(
        functools.partial(_odd_body, lb=lb, has_state=has_state),
        grid=(b, l // lb),
        in_specs=in_specs,
        out_specs=[pl.BlockSpec((1, lb, d), lambda i, j: (i, j, 0)),
                   pl.BlockSpec((1, SUBLANES, conv_dim), lambda i, j: (i, 0, 0)),
                   pl.BlockSpec((1, heads, SSM_HEAD_DIM, D_STATE), lambda i, j: (i, 0, 0, 0))],
        out_shape=[jax.ShapeDtypeStruct(x.shape, F32),
                   jax.ShapeDtypeStruct((b, SUBLANES, conv_dim), F32),
                   jax.ShapeDtypeStruct((b, heads, SSM_HEAD_DIM, D_STATE), F32)],
        scratch_shapes=[pltpu.VMEM((heads // 2, D_STATE, 2 * SSM_HEAD_DIM), F32),
                        pltpu.VMEM((lb, d_inner), BF16),
                        pltpu.VMEM((lb, d_inner), F32),
                        pltpu.VMEM((lb, conv_dim), F32),
                        pltpu.VMEM((lb, d), BF16),
                        pltpu.VMEM((2, SUBLANES + lb, 4 * LANES), F32),
                        pltpu.VMEM((heads, ssd_rows), F32),
                        pltpu.VMEM((ssd_rows, LANES), F32),
                        pltpu.VMEM((heads, ssd_rows), F32),
                        pltpu.VMEM((heads, ssd_rows), F32)],
        compiler_params=pltpu.CompilerParams(
            dimension_semantics=("arbitrary", "arbitrary"), vmem_limit_bytes=VMEM_LIMIT),
        name="odd_mixer",
    )(*args)


def _tail8(state):
    return jnp.pad(state, ((0, 0), (SUBLANES - state.shape[1], 0), (0, 0)))


def _pad_lanes(a):
    return jnp.pad(a, [(0, 0)] * (a.ndim - 1) + [(0, LANES - a.shape[-1])])


def _prepare(p):
    n_even = p["w_in_even"].shape[0]
    n_odd = p["w_in_odd"].shape[0]
    main = 3 * SC_WIDTH + 3 * FOX_WIDTH
    d_inner = p["w_out_odd"].shape[1]
    conv_dim = p["ssm_conv_w"].shape[2]
    row = lambda a: a.reshape(1, -1)
    even = [dict(
        win=p["w_in_even"][j, :, :main].astype(BF16),
        wf=_pad_lanes(p["w_in_even"][j, :, main:]).astype(BF16),
        cw=p["conv_a_w"][j],
        qg=row(jnp.tile(p["q_norm"][j], FOX_HEADS)),
        kg=row(jnp.tile(p["k_norm"][j], FOX_HEADS)),
        bf=_pad_lanes(row(p["b_forget"][j])),
        wout=p["w_out_even"][j].astype(BF16),
    ) for j in range(n_even)]
    odd = [dict(
        wz=p["w_in_odd"][j, :, :d_inner].astype(BF16),
        wx=p["w_in_odd"][j, :, d_inner:d_inner + conv_dim].astype(BF16),
        wdt=_pad_lanes(p["w_in_odd"][j, :, d_inner + conv_dim:]).astype(BF16),
        cw=p["ssm_conv_w"][j],
        cb=row(p["ssm_conv_b"][j]),
        dtb=_pad_lanes(row(p["dt_bias"][j])),
        alog=_pad_lanes(row(p["a_log"][j])),
        dsk=row(jnp.repeat(p["d_skip"][j], SSM_HEAD_DIM)),
        nw=row(p["ssm_norm"][j]),
        wout=p["w_out_odd"][j].astype(BF16),
    ) for j in range(n_odd)]
    ffn = [dict(
        gain=row(p["norm_ffn"][i]),
        wup=p["w_up"][i].astype(BF16),
        cw=p["ffn_conv_w"][i],
        cb=row(p["ffn_conv_b"][i]),
        wdn=p["w_down"][i].astype(BF16),
    ) for i in range(p["w_up"].shape[0])]
    mix_gain = [row(p["norm_mix"][i]) for i in range(p["norm_mix"].shape[0])]
    return even, odd, ffn, mix_gain


def _block_rows(seq):
    return min(EVEN_ROWS, seq), min(ODD_ROWS, seq), min(FFN_ROWS, seq)


def _trunk(x, cache_k, cache_v, cache_logf, st_sconv, st_ssm_conv, st_ssm, st_ffn, prep):
    even, odd, ffn, mix_gain = prep
    lb_even, lb_odd, lb_ffn = _block_rows(x.shape[1])
    b, l, _ = x.shape
    nk, nv, nlf, nsc, nsmc, nsm, nff = [], [], [], [], [], [], []
    for i in range(len(ffn)):
        j = i // 2
        if i % 2 == 0:
            w = even[j]
            cache = None
            if cache_k is not None:
                clft = jnp.swapaxes(cache_logf[j], 1, 2)
                cache = (cache_k[j].reshape(b, -1, FOX_WIDTH), cache_v[j].reshape(b, -1, FOX_WIDTH), clft)
            x, k, v, lf, sc = _even_call(x, cache, _tail8(st_sconv[j]), mix_gain[i], w["win"], w["wf"], w["cw"],
                                         w["qg"], w["kg"], w["bf"], w["wout"], lb_even)
            nk.append(k.reshape(b, l, FOX_HEADS, HEAD_DIM))
            nv.append(v.reshape(b, l, FOX_HEADS, HEAD_DIM))
            nlf.append(lf)
            nsc.append(sc[:, SUBLANES - st_sconv.shape[2]:])
        else:
            w = odd[j]
            x, cs, ss = _odd_call(x, _tail8(st_ssm_conv[j]), None if st_ssm is None else st_ssm[j], mix_gain[i],
                                  w["wz"], w["wx"], w["wdt"], w["cw"], w["cb"], w["dtb"], w["alog"], w["dsk"],
                                  w["nw"], w["wout"], lb_odd)
            nsmc.append(cs[:, SUBLANES - st_ssm_conv.shape[2]:])
            nsm.append(ss)
        w = ffn[i]
        x, fs = _ffn_call(x, _tail8(st_ffn[i]), w["gain"], w["wup"], w["cw"], w["cb"], w["wdn"], lb_ffn)
        nff.append(fs[:, SUBLANES - st_ffn.shape[2]:])
    return (x, jnp.stack(nk), jnp.stack(nv), jnp.stack(nlf), jnp.stack(nsc),
            jnp.stack(nsmc), jnp.stack(nsm), jnp.stack(nff))


def kernel(x_prompt, x_sample, cache_fox_k, cache_fox_v, cache_fox_logf, state_sconv, state_ssm_conv, state_ssm, state_ffn_conv, norm_mix, norm_ffn, w_in_even, conv_a_w, q_norm, k_norm, b_forget, w_out_even, w_in_odd, ssm_conv_w, ssm_conv_b, dt_bias, a_log, d_skip, ssm_norm, w_out_odd, w_up, ffn_conv_w, ffn_conv_b, w_down):
    prep = _prepare(dict(
        norm_mix=norm_mix, norm_ffn=norm_ffn, w_in_even=w_in_even, conv_a_w=conv_a_w, q_norm=q_norm,
        k_norm=k_norm, b_forget=b_forget, w_out_even=w_out_even, w_in_odd=w_in_odd, ssm_conv_w=ssm_conv_w,
        ssm_conv_b=ssm_conv_b, dt_bias=dt_bias, a_log=a_log, d_skip=d_skip, ssm_norm=ssm_norm,
        w_out_odd=w_out_odd, w_up=w_up, ffn_conv_w=ffn_conv_w, ffn_conv_b=ffn_conv_b, w_down=w_down))
    bp = x_prompt.shape[0]
    zeros = lambda s: jnp.zeros((s.shape[0], bp) + s.shape[2:], s.dtype)
    p_out = _trunk(x_prompt, None, None, None, zeros(state_sconv), zeros(state_ssm_conv), None,
                   zeros(state_ffn_conv), prep)
    s_out = _trunk(x_sample, cache_fox_k, cache_fox_v, cache_fox_logf, state_sconv, state_ssm_conv, state_ssm,
                   state_ffn_conv, prep)
    return (p_out[0], s_out[0]) + p_out[1:] + s_out[1:]
```
